```python
import jax, jax.numpy as jnp
from jax import lax
import numpy as np

D_MODEL = 1024
BATCH = 32
SEQ = 2048
DEPTH = 1
DEC_BATCH = 128
DEC_SEQ = 4
PAST_LEN = 8192
PAGE_SIZE = 128

RWKV_HEADS = 8
RWKV_HEAD_DIM = 64
RWKV_WIDTH = RWKV_HEADS * RWKV_HEAD_DIM
DECAY_LORA = 64
AAA_LORA = 64
GATE_LORA = 128
GN_EPS = 64e-5
FOX_HEADS = 8
FOX_HEAD_DIM = 64
FOX_WIDTH = FOX_HEADS * FOX_HEAD_DIM
Q_BLOCK = 128
D_FF = 4 * D_MODEL
LN_EPS = 1e-5
ALPHA = (2.0 * DEPTH) ** 0.25
BETA = (8.0 * DEPTH) ** -0.25

RWKV_PROJ = 3 * RWKV_WIDTH + DECAY_LORA + AAA_LORA + GATE_LORA
FOX_PROJ = 3 * FOX_WIDTH + FOX_HEADS
GATE_PROJ = 2 * D_MODEL
IN_PROJ = RWKV_PROJ + FOX_PROJ + GATE_PROJ
RWKV_SPLITS = (RWKV_WIDTH, 2 * RWKV_WIDTH, 3 * RWKV_WIDTH,
               3 * RWKV_WIDTH + DECAY_LORA, 3 * RWKV_WIDTH + DECAY_LORA + AAA_LORA)
FOX_OFF = RWKV_PROJ
GATE_OFF = RWKV_PROJ + FOX_PROJ

kernel_name = 'rwkv7_fox_griffin_deepnorm_step'


def _layer_norm(x, g, b):
    xf = x.astype(jnp.float32)
    mu = jnp.mean(xf, -1, keepdims=True)
    var = jnp.mean(jnp.square(xf - mu), -1, keepdims=True)
    return ((xf - mu) * lax.rsqrt(var + LN_EPS)).astype(x.dtype) * g + b


def _heads(t, h, d):
    return t.reshape(t.shape[:-1] + (h, d))


def _wkv_scan(r, decay, k, v, kk, a, s0):
    def step(s, inp):
        r_t, w_t, k_t, v_t, kk_t, a_t = inp
        sa = jnp.einsum('bhij,bhj->bhi', s, -kk_t)
        s = (s * w_t[:, :, None, :] + sa[..., None] * (kk_t * a_t)[:, :, None, :]
             + v_t[..., None] * k_t[:, :, None, :])
        return s, jnp.einsum('bhij,bhj->bhi', s, r_t)
    xs = tuple(jnp.moveaxis(t.astype(jnp.float32), 1, 0) for t in (r, decay, k, v, kk, a))
    s, o = lax.scan(step, s0.astype(jnp.float32), xs)
    return jnp.moveaxis(o, 0, 1).astype(r.dtype), s.astype(s0.dtype)


def _rwkv_branch(p, p_prev, s0, mu, w0, w2, a0, a2, g2, k_k, k_a, r_k, gn_g, gn_b):
    B, T, _ = p.shape
    pm = p + (p_prev - p) * mu
    r, k, v, dw, da, dg = jnp.split(pm, RWKV_SPLITS, axis=-1)
    w_log = -jax.nn.softplus(-(w0 + jnp.tanh(dw) @ w2)) - 0.5
    decay = jnp.exp(-jnp.exp(w_log.astype(jnp.float32)))
    a = jax.nn.sigmoid(a0 + da @ a2)
    g = jax.nn.sigmoid(dg) @ g2
    kk = _heads(k * k_k, RWKV_HEADS, RWKV_HEAD_DIM).astype(jnp.float32)
    kk = (kk * lax.rsqrt(jnp.maximum(jnp.sum(kk * kk, -1, keepdims=True), 1e-24))).astype(p.dtype)
    k = k * (1.0 + (a - 1.0) * k_a)
    hd = lambda t: _heads(t, RWKV_HEADS, RWKV_HEAD_DIM)
    r_h, k_h, v_h = hd(r), hd(k), hd(v)
    o, s = _wkv_scan(r_h, hd(decay), k_h, v_h, kk, hd(a), s0)
    of = o.astype(jnp.float32)
    m = jnp.mean(of, -1, keepdims=True)
    var = jnp.mean(jnp.square(of - m), -1, keepdims=True)
    o = ((of - m) * lax.rsqrt(var + GN_EPS)).astype(p.dtype).reshape(B, T, RWKV_WIDTH) * gn_g + gn_b
    bonus = jnp.sum(r_h * k_h * r_k, -1, keepdims=True) * v_h
    return (o + bonus.reshape(B, T, RWKV_WIDTH)) * g, s


def _fox_prompt(q, k, v, logf):
    B, S, H, D = q.shape
    scale = D ** -0.5
    cT = jnp.cumsum(logf, axis=1).transpose(0, 2, 1)
    nb = S // Q_BLOCK
    qb = q.reshape(B, nb, Q_BLOCK, H, D).transpose(1, 0, 2, 3, 4)
    cb = cT.reshape(B, H, nb, Q_BLOCK).transpose(2, 0, 1, 3)
    kpos = jnp.arange(S)

    def block(args):
        i, q_i, c_i = args
        s = jnp.einsum('bqhd,bkhd->bhqk', q_i, k, preferred_element_type=jnp.float32) * scale
        s = s + c_i[..., None] - cT[:, :, None, :]
        qpos = i * Q_BLOCK + jnp.arange(Q_BLOCK)
        s = jnp.where(kpos[None, :] <= qpos[:, None], s, -jnp.inf)
        pr = jax.nn.softmax(s, axis=-1)
        return jnp.einsum('bhqk,bkhd->bqhd', pr.astype(v.dtype), v)

    o = lax.map(block, (jnp.arange(nb), qb, cb))
    return o.transpose(1, 0, 2, 3, 4).reshape(B, S, H * D)


def _fox_sample(q, k, v, logf, k_past, v_past, logf_past):
    B, T, H, D = q.shape
    P = k_past.shape[1]
    scale = D ** -0.5
    c_past = jnp.cumsum(logf_past.astype(jnp.float32), axis=1)
    cnT = jnp.cumsum(logf, axis=1).transpose(0, 2, 1)
    bias_p = (c_past[:, -1:, :] - c_past).transpose(0, 2, 1)[:, :, None, :] + cnT[..., None]
    sp = jnp.einsum('bqhd,bkhd->bhqk', q, k_past, preferred_element_type=jnp.float32) * scale + bias_p
    sn = jnp.einsum('bqhd,bkhd->bhqk', q, k, preferred_element_type=jnp.float32) * scale
    sn = sn + cnT[..., :, None] - cnT[..., None, :]
    sn = jnp.where(jnp.tril(jnp.ones((T, T), bool)), sn, -jnp.inf)
    pr = jax.nn.softmax(jnp.concatenate([sp, sn], axis=-1), axis=-1).astype(v.dtype)
    o = (jnp.einsum('bhqk,bkhd->bqhd', pr[..., :P], v_past)
         + jnp.einsum('bhqk,bkhd->bqhd', pr[..., P:], v))
    return o.reshape(B, T, H * D)


def _layer(x, shift_row, s0, attend, w_in, mu, w0, w2, a0, a2, g2, k_k, k_a, r_k, gn_g, gn_b,
           b_f, w_branch, w_out, ln1_g, ln1_b, w_up, w_down, ln2_g, ln2_b):
    B, T, _ = x.shape
    proj = x @ w_in
    p_r = proj[..., :RWKV_PROJ]
    p_prev = jnp.concatenate([shift_row[:, None, :], p_r[:, :-1]], axis=1)
    o_r, s_new = _rwkv_branch(p_r, p_prev, s0, mu, w0, w2, a0, a2, g2, k_k, k_a, r_k, gn_g, gn_b)
    hf = lambda t: _heads(t, FOX_HEADS, FOX_HEAD_DIM)
    q = hf(proj[..., FOX_OFF:FOX_OFF + FOX_WIDTH])
    k = hf(proj[..., FOX_OFF + FOX_WIDTH:FOX_OFF + 2 * FOX_WIDTH])
    v = hf(proj[..., FOX_OFF + 2 * FOX_WIDTH:FOX_OFF + 3 * FOX_WIDTH])
    logf = jax.nn.log_sigmoid((proj[..., FOX_OFF + 3 * FOX_WIDTH:GATE_OFF] + b_f).astype(jnp.float32))
    o_f = attend(q, k, v, logf)
    gates = jax.nn.sigmoid(proj[..., GATE_OFF:])
    merged = (gates[..., :D_MODEL] * (o_r @ w_branch[:RWKV_WIDTH])
              + gates[..., D_MODEL:] * (o_f @ w_branch[RWKV_WIDTH:]))
    x = _layer_norm(ALPHA * x + merged @ w_out, ln1_g, ln1_b)
    h = jnp.square(jax.nn.relu(x @ w_up)) @ w_down
    x = _layer_norm(ALPHA * x + h, ln2_g, ln2_b)
    return x, s_new, p_r[:, -1], k, v, logf.astype(x.dtype)


def setup_inputs(seed: int = 0) -> dict:
    key = jax.random.key(seed)
    ks = iter(jax.random.split(key, 64))
    f32 = jnp.float32
    nrm = lambda shape, s: jax.random.normal(next(ks), shape, f32) * s
    uni = lambda shape, lo, hi: jax.random.uniform(next(ks), shape, f32, lo, hi)
    n_pages = PAST_LEN // PAGE_SIZE
    n_used = DEC_BATCH * n_pages
    n_pool = (n_used * 5 + 3) // 4
    col_scale = np.full((IN_PROJ,), D_MODEL ** -0.5, np.float32)
    col_scale[2 * RWKV_WIDTH:3 * RWKV_WIDTH] *= BETA
    col_scale[FOX_OFF + 2 * FOX_WIDTH:FOX_OFF + 3 * FOX_WIDTH] *= BETA
    col_scale[FOX_OFF + 3 * FOX_WIDTH:GATE_OFF] *= 0.1
    return {
        'x_prompt': nrm((BATCH, SEQ, D_MODEL), 1.0),
        'x_sample': nrm((DEC_BATCH, DEC_SEQ, D_MODEL), 1.0),
        'state_wkv': nrm((DEPTH, DEC_BATCH, RWKV_HEADS, RWKV_HEAD_DIM, RWKV_HEAD_DIM), 0.3),
        'state_shift': nrm((DEPTH, DEC_BATCH, RWKV_PROJ), 1.0),
        'cache_k': nrm((DEPTH, n_pool, PAGE_SIZE, FOX_HEADS, FOX_HEAD_DIM), 1.0),
        'cache_v': nrm((DEPTH, n_pool, PAGE_SIZE, FOX_HEADS, FOX_HEAD_DIM), BETA),
        'cache_logf': jax.nn.log_sigmoid(uni((DEPTH, n_pool, PAGE_SIZE, FOX_HEADS), 1.0, 6.0)),
        'page_table': jax.random.permutation(next(ks), n_pool)[:n_used].reshape(DEC_BATCH, n_pages).astype(jnp.int32),
        'ln_in_g': 1.0 + nrm((D_MODEL,), 0.05),
        'ln_in_b': nrm((D_MODEL,), 0.02),
        'w_in': nrm((DEPTH, D_MODEL, IN_PROJ), 1.0) * jnp.asarray(col_scale),
        'rwkv_mu': uni((DEPTH, RWKV_PROJ), 0.0, 1.0),
        'rwkv_w0': uni((DEPTH, RWKV_WIDTH), -6.5, -1.5),
        'rwkv_w2': nrm((DEPTH, DECAY_LORA, RWKV_WIDTH), 0.1 * DECAY_LORA ** -0.5),
        'rwkv_a0': nrm((DEPTH, RWKV_WIDTH), 0.5),
        'rwkv_a2': nrm((DEPTH, AAA_LORA, RWKV_WIDTH), 0.5 * AAA_LORA ** -0.5),
        'rwkv_g2': nrm((DEPTH, GATE_LORA, RWKV_WIDTH), GATE_LORA ** -0.5),
        'rwkv_k_k': 0.85 + nrm((DEPTH, RWKV_WIDTH), 0.1),
        'rwkv_k_a': 1.0 + nrm((DEPTH, RWKV_WIDTH), 0.1),
        'rwkv_r_k': nrm((DEPTH, RWKV_HEADS, RWKV_HEAD_DIM), 0.1),
        'rwkv_gn_g': 1.0 + nrm((DEPTH, RWKV_WIDTH), 0.05),
        'rwkv_gn_b': nrm((DEPTH, RWKV_WIDTH), 0.02),
        'fox_b_f': uni((DEPTH, FOX_HEADS), 1.0, 6.0),
        'w_branch': nrm((DEPTH, RWKV_WIDTH + FOX_WIDTH, D_MODEL), BETA * RWKV_WIDTH ** -0.5),
        'w_out': nrm((DEPTH, D_MODEL, D_MODEL), BETA * D_MODEL ** -0.5),
        'ln1_g': 1.0 + nrm((DEPTH, D_MODEL), 0.05),
        'ln1_b': nrm((DEPTH, D_MODEL), 0.02),
        'w_up': nrm((DEPTH, D_MODEL, D_FF), BETA * D_MODEL ** -0.5),
        'w_down': nrm((DEPTH, D_FF, D_MODEL), BETA * D_FF ** -0.5),
        'ln2_g': 1.0 + nrm((DEPTH, D_MODEL), 0.05),
        'ln2_b': nrm((DEPTH, D_MODEL), 0.02),
    }


def reference(x_prompt, x_sample, state_wkv, state_shift, cache_k, cache_v, cache_logf, page_table,
              ln_in_g, ln_in_b, w_in, rwkv_mu, rwkv_w0, rwkv_w2, rwkv_a0, rwkv_a2, rwkv_g2,
              rwkv_k_k, rwkv_k_a, rwkv_r_k, rwkv_gn_g, rwkv_gn_b, fox_b_f, w_branch, w_out,
              ln1_g, ln1_b, w_up, w_down, ln2_g, ln2_b):
    db, n_pages = page_table.shape
    past = n_pages * cache_k.shape[2]
    bp = x_prompt.shape[0]
    hp = _layer_norm(x_prompt, ln_in_g, ln_in_b)
    hs = _layer_norm(x_sample, ln_in_g, ln_in_b)
    pw, psh, pk, pv, pl = [], [], [], [], []
    sw, ssh, sk, sv, sl = [], [], [], [], []
    for l in range(DEPTH):
        lw = (w_in[l], rwkv_mu[l], rwkv_w0[l], rwkv_w2[l], rwkv_a0[l], rwkv_a2[l], rwkv_g2[l],
              rwkv_k_k[l], rwkv_k_a[l], rwkv_r_k[l], rwkv_gn_g[l], rwkv_gn_b[l], fox_b_f[l],
              w_branch[l], w_out[l], ln1_g[l], ln1_b[l], w_up[l], w_down[l], ln2_g[l], ln2_b[l])
        zs = jnp.zeros((bp, RWKV_HEADS, RWKV_HEAD_DIM, RWKV_HEAD_DIM), hp.dtype)
        zr = jnp.zeros((bp, RWKV_PROJ), hp.dtype)
        hp, s_p, sh_p, k_p, v_p, lf_p = _layer(hp, zr, zs, _fox_prompt, *lw)
        k_past = cache_k[l][page_table].reshape(db, past, FOX_HEADS, FOX_HEAD_DIM)
        v_past = cache_v[l][page_table].reshape(db, past, FOX_HEADS, FOX_HEAD_DIM)
        lf_past = cache_logf[l][page_table].reshape(db, past, FOX_HEADS)
        attend = lambda q, k, v, lf, kp=k_past, vp=v_past, lp=lf_past: _fox_sample(q, k, v, lf, kp, vp, lp)
        hs, s_s, sh_s, k_s, v_s, lf_s = _layer(hs, state_shift[l], state_wkv[l], attend, *lw)
        pw.append(s_p); psh.append(sh_p); pk.append(k_p); pv.append(v_p); pl.append(lf_p)
        sw.append(s_s); ssh.append(sh_s); sk.append(k_s); sv.append(v_s); sl.append(lf_s)
    return (hp, hs, jnp.stack(pw), jnp.stack(psh), jnp.stack(pk), jnp.stack(pv), jnp.stack(pl),
            jnp.stack(sw), jnp.stack(ssh), jnp.stack(sk), jnp.stack(sv), jnp.stack(sl))
```

```python
import functools
import math

import numpy as np
import jax
import jax.numpy as jnp
from jax import lax
from jax.experimental import pallas as pl
from jax.experimental.pallas import tpu as pltpu

F32 = jnp.float32
BF16 = jnp.bfloat16

HEADS = 8
HEAD_DIM = 64
WIDTH = HEADS * HEAD_DIM
DECAY_LORA = 64
AAA_LORA = 64
GATE_LORA = 128
RWKV_PROJ = 3 * WIDTH + DECAY_LORA + AAA_LORA + GATE_LORA
LORA_OFF = 3 * WIDTH
GATE_LORA_OFF = LORA_OFF + DECAY_LORA + AAA_LORA
GN_EPS = 64e-5
LN_EPS = 1e-5
QK_SCALE = HEAD_DIM ** -0.5

LANES = 128
CHUNK = 64
GROUP_HEADS = 4
GROUP_W = GROUP_HEADS * HEAD_DIM
N_GROUPS = HEADS // GROUP_HEADS
BIAS_PARTS = 3
BIAS_STRIDE = 8
VMEM_LIMIT = 56 * 1024 * 1024


def _dot(a, b):
    return jnp.dot(a, b, preferred_element_type=F32)


def _dot_nt(a, b):
    return lax.dot_general(a, b, (((1,), (1,)), ((), ())), preferred_element_type=F32)


def _dot_tn(a, b):
    return lax.dot_general(a, b, (((0,), (0,)), ((), ())), preferred_element_type=F32)


def _layer_norm(x, g, b):
    mu = jnp.mean(x, -1, keepdims=True)
    xc = x - mu
    var = jnp.mean(xc * xc, -1, keepdims=True)
    return xc * lax.rsqrt(var + LN_EPS) * g + b


def _split3(x):
    p1 = x.astype(BF16)
    r1 = x - p1.astype(F32)
    p2 = r1.astype(BF16)
    p3 = (r1 - p2.astype(F32)).astype(BF16)
    return p1, p2, p3


def _dot3(m_bf16, x):
    p1, p2, p3 = _split3(x)
    return _dot(m_bf16, p1) + _dot(m_bf16, p2) + _dot(m_bf16, p3)


def _seq_tri(n, seq_len):
    row = lax.broadcasted_iota(jnp.int32, (n, n), 0)
    col = lax.broadcasted_iota(jnp.int32, (n, n), 1)
    keep = col <= row
    if seq_len < n:
        sh = int(math.log2(seq_len))
        assert 1 << sh == seq_len
        keep = keep & (lax.shift_right_logical(row, sh) == lax.shift_right_logical(col, sh))
    return jnp.where(keep, 1.0, 0.0).astype(BF16)


def _const_spec(shape):
    nd = len(shape)
    return pl.BlockSpec(shape, lambda *_: (0,) * nd, pipeline_mode=pl.Buffered(1))


def _params(sem):
    return pltpu.CompilerParams(dimension_semantics=sem, vmem_limit_bytes=VMEM_LIMIT)


def _bias_consts():
    e = np.zeros((2 * BIAS_PARTS, LANES, LANES), np.float32)
    ones_q = np.zeros((1, LANES), np.float32)
    ones_k = np.zeros((1, LANES), np.float32)
    for h in range(HEADS):
        for j in range(BIAS_PARTS):
            e[j, h, BIAS_STRIDE * h + j] = 1.0
            e[BIAS_PARTS + j, h, BIAS_STRIDE * h + BIAS_PARTS + j] = 1.0
            ones_q[0, BIAS_STRIDE * h + BIAS_PARTS + j] = 1.0
            ones_k[0, BIAS_STRIDE * h + j] = 1.0
    return jnp.asarray(e, BF16), jnp.asarray(ones_q), jnp.asarray(ones_k)


def _proj_kernel(x_ref, g_ref, b_ref, wq_ref, wk_ref, wv_ref, wf_ref, bf_ref, wg_ref,
                 e_ref, oq_ref, ok_ref,
                 q_ref, k_ref, v_ref, kb_ref, vb_ref, lf_ref, cq_ref, ck_ref, gt_ref,
                 carry_ref, *, seq_len, tm):
    xn = _layer_norm(x_ref[...], g_ref[...], b_ref[...]).astype(BF16)
    q_ref[...] = (_dot(xn, wq_ref[...]) * QK_SCALE).astype(BF16)
    k = _dot(xn, wk_ref[...])
    k_ref[...] = k
    kb_ref[...] = k.astype(BF16)
    v = _dot(xn, wv_ref[...])
    v_ref[...] = v
    vb_ref[...] = v.astype(BF16)
    gt_ref[...] = jax.nn.sigmoid(_dot(xn, wg_ref[...])).astype(BF16)

    lane = lax.broadcasted_iota(jnp.int32, (tm, LANES), 1)
    lf = jax.nn.log_sigmoid(_dot(xn, wf_ref[...]) + bf_ref[...])
    lf = jnp.where(lane < HEADS, lf, 0.0)
    lf_ref[...] = lf[:, :HEADS]

    c = _dot3(_seq_tri(tm, seq_len), lf)
    if seq_len > tm:
        @pl.when(pl.program_id(0) % (seq_len // tm) == 0)
        def _():
            carry_ref[...] = jnp.zeros_like(carry_ref)
        c = c + carry_ref[...]
        carry_ref[...] = c[tm - 1:tm, :]

    c1, c2, c3 = _split3(c)
    cq = _dot(c1, e_ref[0]) + _dot(c2, e_ref[1]) + _dot(c3, e_ref[2]) + oq_ref[...]
    ck = ok_ref[...] - (_dot(c1, e_ref[3]) + _dot(c2, e_ref[4]) + _dot(c3, e_ref[5]))
    cq_ref[...] = cq.astype(BF16)
    ck_ref[...] = ck.astype(BF16)


def _proj(x2d, ln_g, ln_b, wq, wk, wv, wf, bf, wg, seq_len):
    m, d = x2d.shape
    tm = min(512, m)
    assert m % tm == 0 and (seq_len % tm == 0 or tm % seq_len == 0)
    e, ones_q, ones_k = _bias_consts()
    row = lambda w: pl.BlockSpec((tm, w), lambda i: (i, 0))
    out_shape = (
        jax.ShapeDtypeStruct((m, WIDTH), BF16),
        jax.ShapeDtypeStruct((m, WIDTH), F32),
        jax.ShapeDtypeStruct((m, WIDTH), F32),
        jax.ShapeDtypeStruct((m, WIDTH), BF16),
        jax.ShapeDtypeStruct((m, WIDTH), BF16),
        jax.ShapeDtypeStruct((m, HEADS), F32),
        jax.ShapeDtypeStruct((m, LANES), BF16),
        jax.ShapeDtypeStruct((m, LANES), BF16),
        jax.ShapeDtypeStruct((m, 2 * d), BF16),
    )
    return pl.pallas_call(
        functools.partial(_proj_kernel, seq_len=seq_len, tm=tm),
        out_shape=out_shape,
        grid=(m // tm,),
        in_specs=[row(d), _const_spec((1, d)), _const_spec((1, d)),
                  _const_spec(wq.shape), _const_spec(wk.shape), _const_spec(wv.shape),
                  _const_spec(wf.shape), _const_spec(bf.shape), _const_spec(wg.shape),
                  _const_spec(e.shape), _const_spec(ones_q.shape), _const_spec(ones_k.shape)],
        out_specs=(row(WIDTH), row(WIDTH), row(WIDTH), row(WIDTH), row(WIDTH), row(HEADS),
                   row(LANES), row(LANES), row(2 * d)),
        scratch_shapes=[pltpu.VMEM((1, LANES), F32)],
        compiler_params=_params(("arbitrary",)),
        name="proj",
    )(x2d, ln_g, ln_b, wq, wk, wv, wf, bf, wg, e, ones_q, ones_k)


def _group_mask(rows):
    r = lax.broadcasted_iota(jnp.int32, (rows, GROUP_W), 0)
    c = lax.broadcasted_iota(jnp.int32, (rows, GROUP_W), 1)
    sh = int(math.log2(HEAD_DIM))
    return (lax.shift_right_logical(r, sh) & (GROUP_HEADS - 1)) == lax.shift_right_logical(c, sh)


def _bd(y, mask):
    yb = y.astype(BF16)
    return jnp.where(mask, jnp.concatenate([yb] * GROUP_HEADS, axis=0), jnp.zeros((), BF16))


def _rwkv_kernel(x_ref, g_ref, b_ref, wr_ref, mu_ref, w0_ref, w2_ref, a0_ref, a2_ref, g2_ref,
                 kk_ref, ka_ref, rk_ref, gng_ref, gnb_ref, ones_ref, shift_ref, state_ref,
                 o_ref, shift_out_ref, state_out_ref,
                 s_ref, prev_ref, *, tm, t_real):
    i = pl.program_id(1)
    n_i = pl.num_programs(1)
    mask_bd = _group_mask(GROUP_W)

    @pl.when(i == 0)
    def _():
        prev_ref[...] = shift_ref[0]
        st = state_ref[0]
        for g in range(N_GROUPS):
            blk = st[g * GROUP_W:(g + 1) * GROUP_W, :]
            wide = jnp.concatenate([blk] * GROUP_HEADS, axis=1)
            s_ref[g] = jnp.where(mask_bd, wide, 0.0)

    xn = _layer_norm(x_ref[0], g_ref[...], b_ref[...]).astype(BF16)
    p = _dot(xn, wr_ref[...])
    rowi = lax.broadcasted_iota(jnp.int32, (tm, 1), 0)
    p_prev = jnp.where(rowi == 0, prev_ref[...], pltpu.roll(p, 1, 0))
    last = (t_real - 1) % tm
    prev_ref[...] = p[last:last + 1, :]
    shift_out_ref[0] = p[last:last + 1, :]
    pm = p + (p_prev - p) * mu_ref[...]

    r = pm[:, 0:WIDTH]
    k = pm[:, WIDTH:2 * WIDTH]
    v = pm[:, 2 * WIDTH:3 * WIDTH]
    dwa = pm[:, LORA_OFF:GATE_LORA_OFF]
    dg = pm[:, GATE_LORA_OFF:RWKV_PROJ]
    w_log = -jax.nn.softplus(-(w0_ref[...] + _dot(jnp.tanh(dwa).astype(BF16), w2_ref[...]))) - 0.5
    lw = -jnp.exp(w_log)
    a = jax.nn.sigmoid(a0_ref[...] + _dot(dwa.astype(BF16), a2_ref[...]))
    gate = _dot(jax.nn.sigmoid(dg).astype(BF16), g2_ref[...])
    kk = k * kk_ref[...]
    ss = _dot((kk * kk).astype(BF16), ones_ref[...])
    kk = kk * lax.rsqrt(jnp.maximum(ss, 1e-24))
    k = k * (1.0 + (a - 1.0) * ka_ref[...])
    bonus = _dot((r * k * rk_ref[...]).astype(BF16), ones_ref[...]) * v
    kka = kk * a
    if t_real < tm:
        live = rowi < t_real
        lw = jnp.where(live, lw, 0.0)
        r, k, v, kk, kka = (jnp.where(live, t, 0.0) for t in (r, k, v, kk, kka))

    cl = _dot3(_seq_tri(tm, CHUNK), lw)
    e_in = jnp.exp(cl)
    e_ex = jnp.exp(cl - lw)
    e_ng = jnp.exp(-cl)
    rt = r * e_in
    at = -kk * e_ex
    bt = kka * e_ng
    kt = k * e_ng

    row = lax.broadcasted_iota(jnp.int32, (CHUNK, GROUP_W), 0)
    col = lax.broadcasted_iota(jnp.int32, (CHUNK, GROUP_W), 1) & (CHUNK - 1)
    strict = col < row
    incl = col <= row
    blk16 = lax.shift_right_logical(row, 4) == lax.shift_right_logical(col, 4)
    blk32 = lax.shift_right_logical(row, 5) == lax.shift_right_logical(col, 5)

    def mm(x, y):
        return _dot(x.astype(BF16), _bd(y, mask_bd))

    o_chunks = []
    for c in range(tm // CHUNK):
        rs = slice(c * CHUNK, (c + 1) * CHUNK)
        cl_end = cl[(c + 1) * CHUNK - 1:(c + 1) * CHUNK, :]
        e_end = jnp.exp(cl_end - cl[rs])
        g_end = jnp.exp(cl_end)
        o_groups = []
        for g in range(N_GROUPS):
            ls = slice(g * GROUP_W, (g + 1) * GROUP_W)
            rt_c, at_c, bt_c, kt_c, v_c = rt[rs, ls], at[rs, ls], bt[rs, ls], kt[rs, ls], v[rs, ls]
            ar = jnp.concatenate([at_c, rt_c], axis=0).astype(BF16)
            gb = _dot_nt(ar, _bd(bt_c, mask_bd))
            gk = _dot_nt(ar, _bd(kt_c, mask_bd))
            n_ab = jnp.where(strict, gb[:CHUNK], 0.0)
            a_ak = jnp.where(strict, gk[:CHUNK], 0.0)
            p_rb = jnp.where(incl, gb[CHUNK:], 0.0)
            p_rk = jnp.where(incl, gk[CHUNK:], 0.0)

            n_d = jnp.where(blk16, n_ab, 0.0)
            n_1 = jnp.where(blk32 & jnp.logical_not(blk16), n_ab, 0.0)
            n_2 = jnp.where(blk32, 0.0, n_ab)
            n2 = mm(n_d, n_d)
            n3 = mm(n2, n_d)
            n4 = mm(n2, n2)
            q4 = n_d + n2 + n3
            q8 = q4 + n4 + mm(n4, q4)
            n8 = mm(n4, n4)
            t16 = q8 + n8 + mm(n8, q8)
            x1 = n_1 + mm(n_1, t16)
            t32 = t16 + x1 + mm(t16, x1)
            x2 = n_2 + mm(n_2, t32)
            t = t32 + x2 + mm(t32, x2)

            wt = at_c + mm(t, at_c)
            uv = mm(a_ak, v_c)
            ut = uv + mm(t, uv)

            s0 = s_ref[g]
            ws = _dot_nt(jnp.concatenate([wt, rt_c], axis=0).astype(BF16), s0.astype(BF16))
            u = ws[:CHUNK] + ut
            o_groups.append(ws[CHUNK:] + mm(p_rb, u) + mm(p_rk, v_c))
            uvt = jnp.concatenate([u, v_c], axis=0).astype(BF16)
            bk = jnp.concatenate([kka[rs, ls] * e_end[:, ls], k[rs, ls] * e_end[:, ls]],
                                 axis=0).astype(BF16)
            s_ref[g] = s0 * g_end[:, ls] + jnp.where(mask_bd, _dot_tn(uvt, bk), 0.0)
        o_chunks.append(jnp.concatenate(o_groups, axis=1))
    o = jnp.concatenate(o_chunks, axis=0) if len(o_chunks) > 1 else o_chunks[0]

    inv_n = 1.0 / HEAD_DIM
    mean = _dot(o.astype(BF16), ones_ref[...]) * inv_n
    oc = o - mean
    var = _dot((oc * oc).astype(BF16), ones_ref[...]) * inv_n
    on = oc * lax.rsqrt(var + GN_EPS) * gng_ref[...] + gnb_ref[...]
    o_ref[0] = ((on + bonus) * gate).astype(BF16)

    @pl.when(i == n_i - 1)
    def _():
        for g in range(N_GROUPS):
            sm = jnp.where(mask_bd, s_ref[g], 0.0)
            acc = sm[:, 0:HEAD_DIM]
            for h in range(1, GROUP_HEADS):
                acc = acc + sm[:, h * HEAD_DIM:(h + 1) * HEAD_DIM]
            state_out_ref[0, g * GROUP_W:(g + 1) * GROUP_W, :] = acc


def _rwkv(x3d, t_real, ln_g, ln_b, wr, mu, w0, w2p, a0, a2p, g2, k_k, k_a, r_k, gn_g, gn_b,
          shift_in, state_in):
    nseq, t, d = x3d.shape
    tm = min(256, t)
    assert t % tm == 0 and tm % CHUNK == 0 and (t_real == t or t == tm)
    ones_bd = jnp.asarray(np.kron(np.eye(HEADS, dtype=np.float32),
                                  np.ones((HEAD_DIM, HEAD_DIM), np.float32)), BF16)
    consts = (ln_g, ln_b, wr, mu, w0, w2p, a0, a2p, g2, k_k, k_a, r_k, gn_g, gn_b, ones_bd)
    state2d = state_in.reshape(nseq, HEADS * HEAD_DIM, HEAD_DIM)
    shift3d = shift_in.reshape(nseq, 1, RWKV_PROJ)
    seq_spec = lambda shape: pl.BlockSpec((1,) + shape, lambda b, i: (b, 0, 0))
    o, shift_out, state_out = pl.pallas_call(
        functools.partial(_rwkv_kernel, tm=tm, t_real=t_real),
        out_shape=(jax.ShapeDtypeStruct((nseq, t, WIDTH), BF16),
                   jax.ShapeDtypeStruct((nseq, 1, RWKV_PROJ), F32),
                   jax.ShapeDtypeStruct((nseq, HEADS * HEAD_DIM, HEAD_DIM), F32)),
        grid=(nseq, t // tm),
        in_specs=[pl.BlockSpec((1, tm, d), lambda b, i: (b, i, 0))]
                 + [_const_spec(c.shape) for c in consts]
                 + [seq_spec((1, RWKV_PROJ)), seq_spec((HEADS * HEAD_DIM, HEAD_DIM))],
        out_specs=(pl.BlockSpec((1, tm, WIDTH), lambda b, i: (b, i, 0)),
                   seq_spec((1, RWKV_PROJ)), seq_spec((HEADS * HEAD_DIM, HEAD_DIM))),
        scratch_shapes=[pltpu.VMEM((N_GROUPS, GROUP_W, GROUP_W), F32),
                        pltpu.VMEM((1, RWKV_PROJ), F32)],
        compiler_params=_params(("arbitrary", "arbitrary")),
        name="rwkv",
    )(x3d, *consts, shift3d, state2d)
    return (o, shift_out.reshape(nseq, RWKV_PROJ),
            state_out.reshape(nseq, HEADS, HEAD_DIM, HEAD_DIM))


def _softmax_step(carry, s, v_blk):
    m, l, acc = carry
    m_new = jnp.maximum(m, jnp.max(s, axis=-1, keepdims=True))
    alpha = jnp.exp(m - m_new)
    p = jnp.exp(s - m_new)
    l = alpha * l + jnp.sum(p, axis=-1, keepdims=True)
    acc = alpha * acc + _dot(p.astype(BF16), v_blk)
    return m_new, l, acc


def _fox_prompt_kernel(q_ref, cq_ref, k_ref, v_ref, ck_ref, o_ref, *, tq):
    i = pl.program_id(1)
    lane = lax.broadcasted_iota(jnp.int32, (tq, LANES), 1)
    row = lax.broadcasted_iota(jnp.int32, (tq, tq), 0)
    col = lax.broadcasted_iota(jnp.int32, (tq, tq), 1)
    causal = col <= row
    cq = cq_ref[0]
    zero = jnp.zeros((), BF16)
    for pair in range(HEADS // 2):
        ls = slice(pair * LANES, (pair + 1) * LANES)
        q_pair = q_ref[0, :, ls]
        qa = []
        for e in range(2):
            h = 2 * pair + e
            q_h = jnp.where(lax.shift_right_logical(lane, 6) == e, q_pair, zero)
            c_h = jnp.where(lax.shift_right_logical(lane, 3) == h, cq, zero)
            qa.append(jnp.concatenate([q_h, c_h], axis=1))

        def kv_block(j):
            rs = pl.ds(pl.multiple_of(j * tq, tq), tq)
            ka = jnp.concatenate([k_ref[0, rs, ls], ck_ref[0, rs, :]], axis=1)
            return ka, v_ref[0, rs, ls]

        def body(j, carry):
            ka, v_blk = kv_block(j)
            return tuple(_softmax_step(carry[e], _dot_nt(qa[e], ka), v_blk) for e in range(2))

        init = tuple((jnp.full((tq, 1), -jnp.inf, F32), jnp.zeros((tq, 1), F32),
                      jnp.zeros((tq, LANES), F32)) for _ in range(2))
        carry = lax.fori_loop(0, i, body, init)
        ka, v_blk = kv_block(i)
        outs = []
        for e in range(2):
            s = jnp.where(causal, _dot_nt(qa[e], ka), -jnp.inf)
            _, l, acc = _softmax_step(carry[e], s, v_blk)
            outs.append(acc / l)
        o_ref[0, :, ls] = jnp.where(lax.shift_right_logical(lane, 6) == 0, outs[0], outs[1]).astype(BF16)


def _fox_prompt(q, kb, vb, cq, ck):
    b, s, _ = q.shape
    tq = min(256, s)
    assert s % tq == 0
    blk = lambda w: pl.BlockSpec((1, tq, w), lambda bi, i: (bi, i, 0))
    full = lambda w: pl.BlockSpec((1, s, w), lambda bi, i: (bi, 0, 0))
    return pl.pallas_call(
        functools.partial(_fox_prompt_kernel, tq=tq),
        out_shape=jax.ShapeDtypeStruct((b, s, WIDTH), BF16),
        grid=(b, s // tq),
        in_specs=[blk(WIDTH), blk(LANES), full(WIDTH), full(WIDTH), full(LANES)],
        out_specs=blk(WIDTH),
        compiler_params=_params(("arbitrary", "arbitrary")),
        name="fox_prompt",
    )(q, cq, kb, vb, ck)


def _fox_sample_kernel(pt_ref, q_ref, kn_ref, vn_ref, ckn_ref, *refs, t_new, pps, page):
    k_refs = refs[0:pps]
    v_refs = refs[pps:2 * pps]
    lf_refs = refs[2 * pps:3 * pps]
    o_ref = refs[3 * pps]
    qrow_ref, m_ref, l_ref, acc_ref, run_ref, pad_ref = refs[3 * pps + 1:]
    del pt_ref
    step = pl.program_id(1)
    n_rows = t_new * HEADS
    rowh = lax.broadcasted_iota(jnp.int32, (n_rows, WIDTH), 0) & (HEADS - 1)
    laneh = lax.shift_right_logical(lax.broadcasted_iota(jnp.int32, (n_rows, WIDTH), 1), 6)
    head_mask = rowh == laneh

    @pl.when(step == 0)
    def _():
        q = q_ref[0]
        rows = jnp.concatenate([jnp.broadcast_to(q[t:t + 1, :], (HEADS, WIDTH)) for t in range(t_new)],
                               axis=0)
        qrow_ref[...] = jnp.where(head_mask, rows, 0.0).astype(BF16)
        m_ref[...] = jnp.full(m_ref.shape, -jnp.inf, F32)
        l_ref[...] = jnp.zeros(l_ref.shape, F32)
        acc_ref[...] = jnp.zeros(acc_ref.shape, F32)
        run_ref[...] = jnp.zeros(run_ref.shape, F32)

    qrows = qrow_ref[...]
    lf = jnp.concatenate([r[0] for r in lf_refs], axis=0)
    lane = lax.broadcasted_iota(jnp.int32, lf.shape, 1)
    suf = lf
    sh = 1
    while sh < page:
        suf = suf + jnp.where(lane + sh < page, pltpu.roll(suf, page - sh, 1), 0.0)
        sh *= 2
    carry = (m_ref[...], l_ref[...], acc_ref[...])
    run = run_ref[...]
    for n in range(pps):
        blk = slice(n * HEADS, (n + 1) * HEADS)
        bias = suf[blk] - lf[blk] + run
        run = run + suf[blk][:, 0:1]
        s = _dot_nt(qrows, k_refs[n][0].astype(BF16))
        s = s + jnp.concatenate([bias] * t_new, axis=0)
        carry = _softmax_step(carry, s, v_refs[n][0].astype(BF16))
    m_ref[...], l_ref[...], acc_ref[...] = carry
    run_ref[...] = run

    @pl.when(step == pl.num_programs(1) - 1)
    def _():
        pad_ref[...] = jnp.zeros(pad_ref.shape, F32)
        pad_ref[0, 0:t_new, :] = kn_ref[0]
        pad_ref[1, 0:t_new, :] = vn_ref[0]
        pad_ref[2, 0:t_new, 0:LANES] = ckn_ref[0]
        kn = pad_ref[0].astype(BF16)
        vn = pad_ref[1].astype(BF16)
        ckn = pad_ref[2, :, 0:LANES].astype(BF16)
        r8 = lax.broadcasted_iota(jnp.int32, (n_rows, LANES), 0) & (HEADS - 1)
        l8 = lax.broadcasted_iota(jnp.int32, (n_rows, LANES), 1)
        sel = (lax.shift_right_logical(l8, 3) == r8) & ((l8 & (BIAS_STRIDE - 1)) >= BIAS_PARTS) \
            & ((l8 & (BIAS_STRIDE - 1)) < 2 * BIAS_PARTS)
        cqr = jnp.where(sel, 1.0, 0.0).astype(BF16)
        s = _dot_nt(jnp.concatenate([qrows, cqr], axis=1), jnp.concatenate([kn, ckn], axis=1))
        key = lax.broadcasted_iota(jnp.int32, (n_rows, page), 1)
        tok = lax.shift_right_logical(lax.broadcasted_iota(jnp.int32, (n_rows, page), 0), 3)
        s = jnp.where(key <= tok, s, -jnp.inf)
        _, l, acc = _softmax_step((m_ref[...], l_ref[...], acc_ref[...]), s, vn)
        o = jnp.where(head_mask, acc / l, 0.0)
        o_ref[0] = jnp.sum(o.reshape(t_new, HEADS, WIDTH), axis=1).astype(BF16)


def _fox_sample(q, kb, vb, ck, cache_k, cache_v, cache_lf_t, page_table):
    db, t_new, _ = q.shape
    n_pages = page_table.shape[1]
    n_pool, page, _ = cache_k.shape
    pps = 8
    assert n_pages % pps == 0 and page == LANES and t_new <= HEADS
    n_rows = t_new * HEADS
    pt = page_table.reshape(-1)

    def page_idx(n):
        return lambda b, s, pt_ref: (pt_ref[b * n_pages + n_pages - 1 - (s * pps + n)], 0, 0)

    new = lambda w: pl.BlockSpec((1, t_new, w), lambda b, s, pt_ref: (b, 0, 0))
    in_specs = ([new(WIDTH), new(WIDTH), new(WIDTH), new(LANES)]
                + [pl.BlockSpec((1, page, WIDTH), page_idx(n)) for n in range(pps)]
                + [pl.BlockSpec((1, page, WIDTH), page_idx(n)) for n in range(pps)]
                + [pl.BlockSpec((1, HEADS, page), page_idx(n)) for n in range(pps)])
    grid_spec = pltpu.PrefetchScalarGridSpec(
        num_scalar_prefetch=1,
        grid=(db, n_pages // pps),
        in_specs=in_specs,
        out_specs=new(WIDTH),
        scratch_shapes=[pltpu.VMEM((n_rows, WIDTH), BF16),
                        pltpu.VMEM((n_rows, 1), F32), pltpu.VMEM((n_rows, 1), F32),
                        pltpu.VMEM((n_rows, WIDTH), F32), pltpu.VMEM((HEADS, 1), F32),
                        pltpu.VMEM((3, page, WIDTH), F32)],
    )
    return pl.pallas_call(
        functools.partial(_fox_sample_kernel, t_new=t_new, pps=pps, page=page),
        out_shape=jax.ShapeDtypeStruct((db, t_new, WIDTH), BF16),
        grid_spec=grid_spec,
        compiler_params=_params(("arbitrary", "arbitrary")),
        name="fox_sample",
    )(pt, q, kb, vb, ck, *([cache_k] * pps), *([cache_v] * pps), *([cache_lf_t] * pps))


def _mlp_kernel(x_ref, or_ref, of_ref, gt_ref, lg_ref, lb_ref, wb_ref, wo_ref, g1_ref, b1_ref,
                wu_ref, wd_ref, g2_ref, b2_ref, y_ref, *, alpha, ff_chunk):
    d = x_ref.shape[1]
    hp = _layer_norm(x_ref[...], lg_ref[...], lb_ref[...])
    gt = gt_ref[...]
    merged = (gt[:, :d].astype(F32) * _dot(or_ref[...], wb_ref[0:WIDTH, :])
              + gt[:, d:].astype(F32) * _dot(of_ref[...], wb_ref[WIDTH:2 * WIDTH, :]))
    x1 = _layer_norm(alpha * hp + _dot(merged.astype(BF16), wo_ref[...]), g1_ref[...], b1_ref[...])
    x1b = x1.astype(BF16)
    h = jnp.zeros_like(x1)
    for c in range(wu_ref.shape[1] // ff_chunk):
        cs = slice(c * ff_chunk, (c + 1) * ff_chunk)
        up = jnp.maximum(_dot(x1b, wu_ref[:, cs]), 0.0)
        h = h + _dot((up * up).astype(BF16), wd_ref[cs, :])
    y_ref[...] = _layer_norm(alpha * x1 + h, g2_ref[...], b2_ref[...])


def _mlp(x2d, o_r, o_f, gates, ln_g, ln_b, wb, wo, g1, b1, wu, wd, g2, b2, alpha):
    m, d = x2d.shape
    tm = min(512, m)
    assert m % tm == 0
    row = lambda w: pl.BlockSpec((tm, w), lambda i: (i, 0))
    consts = (ln_g, ln_b, wb, wo, g1, b1, wu, wd, g2, b2)
    return pl.pallas_call(
        functools.partial(_mlp_kernel, alpha=alpha, ff_chunk=1024),
        out_shape=jax.ShapeDtypeStruct((m, d), F32),
        grid=(m // tm,),
        in_specs=[row(d), row(WIDTH), row(WIDTH), row(2 * d)] + [_const_spec(c.shape) for c in consts],
        out_specs=row(d),
        compiler_params=_params(("arbitrary",)),
        name="mlp",
    )(x2d, o_r, o_f, gates, *consts)


def kernel(x_prompt, x_sample, state_wkv, state_shift, cache_k, cache_v, cache_logf, page_table, ln_in_g, ln_in_b, w_in, rwkv_mu, rwkv_w0, rwkv_w2, rwkv_a0, rwkv_a2, rwkv_g2, rwkv_k_k, rwkv_k_a, rwkv_r_k, rwkv_gn_g, rwkv_gn_b, fox_b_f, w_branch, w_out, ln1_g, ln1_b, w_up, w_down, ln2_g, ln2_b):
    depth = w_in.shape[0]
    assert depth == 1, "the entry LayerNorm is fused into the first layer's kernels"
    bp, seq, d = x_prompt.shape
    db, dseq, _ = x_sample.shape
    n_pool, page = cache_k.shape[1], cache_k.shape[2]
    alpha = (2.0 * depth) ** 0.25
    row2 = lambda t: t.reshape(1, -1).astype(F32)
    ln_g, ln_b = row2(ln_in_g), row2(ln_in_b)

    l = 0
    w = w_in[l]
    fox_off = RWKV_PROJ
    wr = w[:, :RWKV_PROJ].astype(BF16)
    wq = w[:, fox_off:fox_off + WIDTH].astype(BF16)
    wk = w[:, fox_off + WIDTH:fox_off + 2 * WIDTH].astype(BF16)
    wv = w[:, fox_off + 2 * WIDTH:fox_off + 3 * WIDTH].astype(BF16)
    wf = jnp.pad(w[:, fox_off + 3 * WIDTH:fox_off + 3 * WIDTH + HEADS], ((0, 0), (0, LANES - HEADS))).astype(BF16)
    wg = w[:, fox_off + 3 * WIDTH + HEADS:].astype(BF16)
    bf = jnp.pad(fox_b_f[l].reshape(1, HEADS), ((0, 0), (0, LANES - HEADS))).astype(F32)
    zl = jnp.zeros((DECAY_LORA, WIDTH), BF16)
    w2p = jnp.concatenate([rwkv_w2[l].astype(BF16), zl], axis=0)
    a2p = jnp.concatenate([zl, rwkv_a2[l].astype(BF16)], axis=0)
    rwkv_consts = (ln_g, ln_b, wr, row2(rwkv_mu[l]), row2(rwkv_w0[l]), w2p, row2(rwkv_a0[l]), a2p,
                   rwkv_g2[l].astype(BF16), row2(rwkv_k_k[l]), row2(rwkv_k_a[l]), row2(rwkv_r_k[l]),
                   row2(rwkv_gn_g[l]), row2(rwkv_gn_b[l]))
    mlp_consts = (ln_g, ln_b, w_branch[l].astype(BF16), w_out[l].astype(BF16), row2(ln1_g[l]),
                  row2(ln1_b[l]), w_up[l].astype(BF16), w_down[l].astype(BF16), row2(ln2_g[l]),
                  row2(ln2_b[l]))

    xp2 = x_prompt.reshape(bp * seq, d)
    q, k32, v32, kb, vb, lf, cq, ck, gates = _proj(xp2, ln_g, ln_b, wq, wk, wv, wf, bf, wg, seq)
    to_seq = lambda t: t.reshape(bp, seq, t.shape[-1])
    o_f = _fox_prompt(to_seq(q), to_seq(kb), to_seq(vb), to_seq(cq), to_seq(ck))
    o_r, p_shift, p_wkv = _rwkv(x_prompt, seq, *rwkv_consts,
                                jnp.zeros((bp, RWKV_PROJ), F32),
                                jnp.zeros((bp, HEADS, HEAD_DIM, HEAD_DIM), F32))
    y_prompt = _mlp(xp2, o_r.reshape(bp * seq, WIDTH), o_f.reshape(bp * seq, WIDTH), gates,
                    *mlp_consts, alpha).reshape(bp, seq, d)
    p_k = k32.reshape(1, bp, seq, HEADS, HEAD_DIM)
    p_v = v32.reshape(1, bp, seq, HEADS, HEAD_DIM)
    p_lf = lf.reshape(1, bp, seq, HEADS)

    xs2 = x_sample.reshape(db * dseq, d)
    q, k32, v32, kb, vb, lf, cq, ck, gates = _proj(xs2, ln_g, ln_b, wq, wk, wv, wf, bf, wg, dseq)
    to_seq = lambda t: t.reshape(db, dseq, t.shape[-1])
    o_f = _fox_sample(to_seq(q.astype(F32)), to_seq(k32), to_seq(v32), to_seq(ck.astype(F32)),
                      cache_k[l].reshape(n_pool, page, WIDTH), cache_v[l].reshape(n_pool, page, WIDTH),
                      jnp.swapaxes(cache_logf[l], 1, 2), page_table)
    xs_pad = jnp.pad(x_sample, ((0, 0), (0, CHUNK - dseq), (0, 0)))
    o_r, s_shift, s_wkv = _rwkv(xs_pad, dseq, *rwkv_consts, state_shift[l], state_wkv[l])
    y_sample = _mlp(xs2, o_r[:, :dseq].reshape(db * dseq, WIDTH), o_f.reshape(db * dseq, WIDTH), gates,
                    *mlp_consts, alpha).reshape(db, dseq, d)
    s_k = k32.reshape(1, db, dseq, HEADS, HEAD_DIM)
    s_v = v32.reshape(1, db, dseq, HEADS, HEAD_DIM)
    s_lf = lf.reshape(1, db, dseq, HEADS)

    return (y_prompt, y_sample, p_wkv[None], p_shift[None], p_k, p_v, p_lf,
            s_wkv[None], s_shift[None], s_k, s_v, s_lf)
```

```python
import functools
import math

import numpy as np
import jax
import jax.numpy as jnp
from jax import lax
from jax.experimental import pallas as pl
from jax.experimental.pallas import tpu as pltpu

F32 = jnp.float32
BF16 = jnp.bfloat16

HEADS = 8
HEAD_DIM = 64
WIDTH = HEADS * HEAD_DIM
DECAY_LORA = 64
AAA_LORA = 64
GATE_LORA = 128
RWKV_PROJ = 3 * WIDTH + DECAY_LORA + AAA_LORA + GATE_LORA
LORA_OFF = 3 * WIDTH
GATE_LORA_OFF = LORA_OFF + DECAY_LORA + AAA_LORA
GN_EPS = 64e-5
LN_EPS = 1e-5
QK_SCALE = HEAD_DIM ** -0.5

LANES = 128
CHUNK = 64
GROUP_HEADS = 4
GROUP_W = GROUP_HEADS * HEAD_DIM
N_GROUPS = HEADS // GROUP_HEADS
BIAS_PARTS = 3
BIAS_STRIDE = 8
VMEM_LIMIT = 56 * 1024 * 1024


def _dot(a, b):
    return jnp.dot(a, b, preferred_element_type=F32)


def _dot_nt(a, b):
    return lax.dot_general(a, b, (((1,), (1,)), ((), ())), preferred_element_type=F32)


def _dot_tn(a, b):
    return lax.dot_general(a, b, (((0,), (0,)), ((), ())), preferred_element_type=F32)


def _layer_norm(x, g, b):
    mu = jnp.mean(x, -1, keepdims=True)
    xc = x - mu
    var = jnp.mean(xc * xc, -1, keepdims=True)
    return xc * lax.rsqrt(var + LN_EPS) * g + b


def _split3(x):
    p1 = x.astype(BF16)
    r1 = x - p1.astype(F32)
    p2 = r1.astype(BF16)
    p3 = (r1 - p2.astype(F32)).astype(BF16)
    return p1, p2, p3


def _dot3(m_bf16, x):
    p1, p2, p3 = _split3(x)
    return _dot(m_bf16, p1) + _dot(m_bf16, p2) + _dot(m_bf16, p3)


def _dot3_rhs(x, m_bf16):
    p1, p2, p3 = _split3(x)
    return _dot(p1, m_bf16) + _dot(p2, m_bf16) + _dot(p3, m_bf16)


def _seq_tri(n, seq_len):
    row = lax.broadcasted_iota(jnp.int32, (n, n), 0)
    col = lax.broadcasted_iota(jnp.int32, (n, n), 1)
    keep = col <= row
    if seq_len < n:
        sh = int(math.log2(seq_len))
        assert 1 << sh == seq_len
        keep = keep & (lax.shift_right_logical(row, sh) == lax.shift_right_logical(col, sh))
    return jnp.where(keep, 1.0, 0.0).astype(BF16)


def _const_spec(shape):
    nd = len(shape)
    return pl.BlockSpec(shape, lambda *_: (0,) * nd, pipeline_mode=pl.Buffered(1))


def _params(sem):
    return pltpu.CompilerParams(dimension_semantics=sem, vmem_limit_bytes=VMEM_LIMIT)


def _bias_consts():
    e = np.zeros((2 * BIAS_PARTS, LANES, LANES), np.float32)
    ones_q = np.zeros((1, LANES), np.float32)
    ones_k = np.zeros((1, LANES), np.float32)
    for h in range(HEADS):
        for j in range(BIAS_PARTS):
            e[j, h, BIAS_STRIDE * h + j] = 1.0
            e[BIAS_PARTS + j, h, BIAS_STRIDE * h + BIAS_PARTS + j] = 1.0
            ones_q[0, BIAS_STRIDE * h + BIAS_PARTS + j] = 1.0
            ones_k[0, BIAS_STRIDE * h + j] = 1.0
    return jnp.asarray(e, BF16), jnp.asarray(ones_q), jnp.asarray(ones_k)


def _proj_kernel(x_ref, g_ref, b_ref, wq_ref, wk_ref, wv_ref, wf_ref, bf_ref, wg_ref,
                 e_ref, oq_ref, ok_ref,
                 q_ref, k_ref, v_ref, kb_ref, vb_ref, lf_ref, cq_ref, ck_ref, gt_ref,
                 carry_ref, *, seq_len, tm):
    xn = _layer_norm(x_ref[...], g_ref[...], b_ref[...]).astype(BF16)
    q_ref[...] = (_dot(xn, wq_ref[...]) * QK_SCALE).astype(BF16)
    k = _dot(xn, wk_ref[...])
    k_ref[...] = k
    kb_ref[...] = k.astype(BF16)
    v = _dot(xn, wv_ref[...])
    v_ref[...] = v
    vb_ref[...] = v.astype(BF16)
    gt_ref[...] = jax.nn.sigmoid(_dot(xn, wg_ref[...])).astype(BF16)

    lane = lax.broadcasted_iota(jnp.int32, (tm, LANES), 1)
    lf = jax.nn.log_sigmoid(_dot(xn, wf_ref[...]) + bf_ref[...])
    lf = jnp.where(lane < HEADS, lf, 0.0)
    lf_ref[...] = lf[:, :HEADS]

    c = _dot3(_seq_tri(tm, seq_len), lf)
    if seq_len > tm:
        @pl.when(pl.program_id(0) % (seq_len // tm) == 0)
        def _():
            carry_ref[...] = jnp.zeros_like(carry_ref)
        c = c + carry_ref[...]
        carry_ref[...] = c[tm - 1:tm, :]

    c1, c2, c3 = _split3(c)
    cq = _dot(c1, e_ref[0]) + _dot(c2, e_ref[1]) + _dot(c3, e_ref[2]) + oq_ref[...]
    ck = ok_ref[...] - (_dot(c1, e_ref[3]) + _dot(c2, e_ref[4]) + _dot(c3, e_ref[5]))
    cq_ref[...] = cq.astype(BF16)
    ck_ref[...] = ck.astype(BF16)


def _proj(x2d, ln_g, ln_b, wq, wk, wv, wf, bf, wg, seq_len):
    m, d = x2d.shape
    tm = min(512, m)
    assert m % tm == 0 and (seq_len % tm == 0 or tm % seq_len == 0)
    e, ones_q, ones_k = _bias_consts()
    row = lambda w: pl.BlockSpec((tm, w), lambda i: (i, 0))
    out_shape = (
        jax.ShapeDtypeStruct((m, WIDTH), BF16),
        jax.ShapeDtypeStruct((m, WIDTH), F32),
        jax.ShapeDtypeStruct((m, WIDTH), F32),
        jax.ShapeDtypeStruct((m, WIDTH), BF16),
        jax.ShapeDtypeStruct((m, WIDTH), BF16),
        jax.ShapeDtypeStruct((m, HEADS), F32),
        jax.ShapeDtypeStruct((m, LANES), BF16),
        jax.ShapeDtypeStruct((m, LANES), BF16),
        jax.ShapeDtypeStruct((m, 2 * d), BF16),
    )
    return pl.pallas_call(
        functools.partial(_proj_kernel, seq_len=seq_len, tm=tm),
        out_shape=out_shape,
        grid=(m // tm,),
        in_specs=[row(d), _const_spec((1, d)), _const_spec((1, d)),
                  _const_spec(wq.shape), _const_spec(wk.shape), _const_spec(wv.shape),
                  _const_spec(wf.shape), _const_spec(bf.shape), _const_spec(wg.shape),
                  _const_spec(e.shape), _const_spec(ones_q.shape), _const_spec(ones_k.shape)],
        out_specs=(row(WIDTH), row(WIDTH), row(WIDTH), row(WIDTH), row(WIDTH), row(HEADS),
                   row(LANES), row(LANES), row(2 * d)),
        scratch_shapes=[pltpu.VMEM((1, LANES), F32)],
        compiler_params=_params(("arbitrary",)),
        name="proj",
    )(x2d, ln_g, ln_b, wq, wk, wv, wf, bf, wg, e, ones_q, ones_k)


def _group_mask(rows):
    r = lax.broadcasted_iota(jnp.int32, (rows, GROUP_W), 0)
    c = lax.broadcasted_iota(jnp.int32, (rows, GROUP_W), 1)
    sh = int(math.log2(HEAD_DIM))
    return (lax.shift_right_logical(r, sh) & (GROUP_HEADS - 1)) == lax.shift_right_logical(c, sh)


def _bd(y, mask):
    yb = y.astype(BF16)
    return jnp.where(mask, jnp.concatenate([yb] * GROUP_HEADS, axis=0), jnp.zeros((), BF16))


def _state_to_bd(st, mask_bd):
    out = []
    for g in range(N_GROUPS):
        blk = st[g * GROUP_W:(g + 1) * GROUP_W, :]
        out.append(jnp.where(mask_bd, jnp.concatenate([blk] * GROUP_HEADS, axis=1), 0.0))
    return out


def _state_from_bd(state, mask_bd):
    rows = []
    for sg in state:
        sm = jnp.where(mask_bd, sg, 0.0)
        acc = sm[:, 0:HEAD_DIM]
        for h in range(1, GROUP_HEADS):
            acc = acc + sm[:, h * HEAD_DIM:(h + 1) * HEAD_DIM]
        rows.append(acc)
    return jnp.concatenate(rows, axis=0)


def _rwkv_kernel(x_ref, g_ref, b_ref, wr_ref, mu_ref, w0_ref, w2_ref, a0_ref, a2_ref, g2_ref,
                 kk_ref, ka_ref, rk_ref, gng_ref, gnb_ref, ones_ref, shift_ref, state_ref,
                 o_ref, shift_out_ref, state_out_ref,
                 s_ref, prev_ref, *, tm, t_real, chained):
    nc = tm // CHUNK
    mask_bd = _group_mask(GROUP_W)
    rowi = lax.broadcasted_iota(jnp.int32, (tm, 1), 0)

    if chained:
        @pl.when(pl.program_id(1) == 0)
        def _():
            prev_ref[...] = shift_ref[0]
            for g, sg in enumerate(_state_to_bd(state_ref[0], mask_bd)):
                s_ref[g] = sg

    xn = _layer_norm(x_ref[...].reshape(tm, x_ref.shape[-1]), g_ref[...], b_ref[...]).astype(BF16)
    p = _dot(xn, wr_ref[...])
    if chained:
        p_prev = jnp.where(rowi == 0, prev_ref[...], pltpu.roll(p, 1, 0))
        prev_ref[...] = p[tm - 1:tm, :]
        shift_out_ref[0] = p[tm - 1:tm, :]
    else:
        first = jnp.concatenate([jnp.broadcast_to(shift_ref[c], (CHUNK, RWKV_PROJ)) for c in range(nc)], axis=0)
        p_prev = jnp.where((rowi & (CHUNK - 1)) == 0, first, pltpu.roll(p, 1, 0))
        for c in range(nc):
            shift_out_ref[c] = p[c * CHUNK + t_real - 1:c * CHUNK + t_real, :]
    pm = p + (p_prev - p) * mu_ref[...]

    r = pm[:, 0:WIDTH]
    k = pm[:, WIDTH:2 * WIDTH]
    v = pm[:, 2 * WIDTH:3 * WIDTH]
    dwa = pm[:, LORA_OFF:GATE_LORA_OFF]
    dg = pm[:, GATE_LORA_OFF:RWKV_PROJ]
    w_log = -jax.nn.softplus(-(w0_ref[...] + _dot(jnp.tanh(dwa).astype(BF16), w2_ref[...]))) - 0.5
    lw = -jnp.exp(w_log)
    a = jax.nn.sigmoid(a0_ref[...] + _dot(dwa.astype(BF16), a2_ref[...]))
    gate = _dot(jax.nn.sigmoid(dg).astype(BF16), g2_ref[...])
    kk = k * kk_ref[...]
    ss = _dot((kk * kk).astype(BF16), ones_ref[...])
    kk = kk * lax.rsqrt(jnp.maximum(ss, 1e-24))
    k = k * (1.0 + (a - 1.0) * ka_ref[...])
    bonus = _dot((r * k * rk_ref[...]).astype(BF16), ones_ref[...]) * v
    kka = kk * a
    if not chained and t_real < CHUNK:
        live = (rowi & (CHUNK - 1)) < t_real
        lw = jnp.where(live, lw, 0.0)
        r, k, v, kk, kka = (jnp.where(live, t, 0.0) for t in (r, k, v, kk, kka))

    cl = _dot3(_seq_tri(tm, CHUNK), lw)
    e_in = jnp.exp(cl)
    e_ex = jnp.exp(cl - lw)
    e_ng = jnp.exp(-cl)
    rt = r * e_in
    at = -kk * e_ex
    bt = kka * e_ng
    kt = k * e_ng

    row = lax.broadcasted_iota(jnp.int32, (CHUNK, GROUP_W), 0)
    col = lax.broadcasted_iota(jnp.int32, (CHUNK, GROUP_W), 1) & (CHUNK - 1)
    strict = col < row
    incl = col <= row
    blk16 = lax.shift_right_logical(row, 4) == lax.shift_right_logical(col, 4)
    blk32 = lax.shift_right_logical(row, 5) == lax.shift_right_logical(col, 5)

    def mm(x, y):
        return _dot(x.astype(BF16), _bd(y, mask_bd))

    probs = [(c, g) for c in range(nc) for g in range(N_GROUPS)]
    sl = lambda x, c, g: x[c * CHUNK:(c + 1) * CHUNK, g * GROUP_W:(g + 1) * GROUP_W]
    each = lambda f, *ls: [f(*xs) for xs in zip(*ls)]
    rt_, at_, bt_, kt_, v_ = ([sl(x, c, g) for c, g in probs] for x in (rt, at, bt, kt, v))
    cl_end = [cl[(c + 1) * CHUNK - 1:(c + 1) * CHUNK, :] for c in range(nc)]
    g_end = [jnp.exp(ce) for ce in cl_end]
    e_end = [jnp.exp(cl_end[c] - cl[c * CHUNK:(c + 1) * CHUNK, :]) for c in range(nc)]
    bh_ = [sl(kka, c, g) * e_end[c][:, g * GROUP_W:(g + 1) * GROUP_W] for c, g in probs]
    kh_ = [sl(k, c, g) * e_end[c][:, g * GROUP_W:(g + 1) * GROUP_W] for c, g in probs]

    ar_ = each(lambda a_, r_: jnp.concatenate([a_, r_], axis=0).astype(BF16), at_, rt_)
    gb_ = each(lambda ar, b_: _dot_nt(ar, _bd(b_, mask_bd)), ar_, bt_)
    gk_ = each(lambda ar, k_: _dot_nt(ar, _bd(k_, mask_bd)), ar_, kt_)
    n_ab = [jnp.where(strict, x[:CHUNK], 0.0) for x in gb_]
    a_ak = [jnp.where(strict, x[:CHUNK], 0.0) for x in gk_]
    p_rb = [jnp.where(incl, x[CHUNK:], 0.0) for x in gb_]
    p_rk = [jnp.where(incl, x[CHUNK:], 0.0) for x in gk_]
    uv_ = each(mm, a_ak, v_)

    n_d = [jnp.where(blk16, x, 0.0) for x in n_ab]
    n_1 = [jnp.where(blk32 & jnp.logical_not(blk16), x, 0.0) for x in n_ab]
    n_2 = [jnp.where(blk32, 0.0, x) for x in n_ab]
    add = lambda *xs: functools.reduce(lambda a_, b_: a_ + b_, xs)
    n2 = each(mm, n_d, n_d)
    n3 = each(mm, n2, n_d)
    n4 = each(mm, n2, n2)
    q4 = each(add, n_d, n2, n3)
    q8 = each(add, q4, n4, each(mm, n4, q4))
    n8 = each(mm, n4, n4)
    t16 = each(add, q8, n8, each(mm, n8, q8))
    x1 = each(add, n_1, each(mm, n_1, t16))
    t32 = each(add, t16, x1, each(mm, t16, x1))
    x2 = each(add, n_2, each(mm, n_2, t32))
    t_ = each(add, t32, x2, each(mm, t32, x2))

    wt_ = each(add, at_, each(mm, t_, at_))
    ut_ = each(add, uv_, each(mm, t_, uv_))
    mw_ = each(lambda w_, b_: jnp.where(mask_bd, _dot_tn(w_.astype(BF16), b_.astype(BF16)), 0.0).astype(BF16),
               wt_, bh_)
    dd_ = each(lambda u_, vv, b_, k_: jnp.where(mask_bd, _dot_tn(
        jnp.concatenate([u_, vv], axis=0).astype(BF16), jnp.concatenate([b_, k_], axis=0).astype(BF16)), 0.0),
               ut_, v_, bh_, kh_)
    qq_ = each(lambda r_, x: (r_ + x).astype(BF16), rt_, each(mm, p_rb, wt_))
    o0_ = each(add, each(mm, p_rb, ut_), each(mm, p_rk, v_))

    if chained:
        state = [s_ref[g] for g in range(N_GROUPS)]
    o_chunks = []
    for c in range(nc):
        if not chained:
            state = _state_to_bd(state_ref[c], mask_bd)
        o_groups = []
        for g in range(N_GROUPS):
            j = c * N_GROUPS + g
            s0 = state[g]
            s0b = s0.astype(BF16)
            o_groups.append(_dot_nt(qq_[j], s0b) + o0_[j])
            state[g] = s0 * g_end[c][:, g * GROUP_W:(g + 1) * GROUP_W] + _dot(s0b, mw_[j]) + dd_[j]
        o_chunks.append(jnp.concatenate(o_groups, axis=1))
        if not chained:
            state_out_ref[c] = _state_from_bd(state, mask_bd)
    if chained:
        for g in range(N_GROUPS):
            s_ref[g] = state[g]
    o = jnp.concatenate(o_chunks, axis=0) if len(o_chunks) > 1 else o_chunks[0]

    inv_n = 1.0 / HEAD_DIM
    mean = _dot(o.astype(BF16), ones_ref[...]) * inv_n
    oc = o - mean
    var = _dot((oc * oc).astype(BF16), ones_ref[...]) * inv_n
    on = oc * lax.rsqrt(var + GN_EPS) * gng_ref[...] + gnb_ref[...]
    o_ref[...] = ((on + bonus) * gate).astype(BF16).reshape(o_ref.shape)

    if chained:
        @pl.when(pl.program_id(1) == pl.num_programs(1) - 1)
        def _():
            state_out_ref[0] = _state_from_bd([s_ref[g] for g in range(N_GROUPS)], mask_bd)


def _rwkv(x3d, t_real, ln_g, ln_b, wr, mu, w0, w2p, a0, a2p, g2, k_k, k_a, r_k, gn_g, gn_b,
          shift_in, state_in):
    nseq, t, d = x3d.shape
    chained = t_real == t
    if chained:
        tm, nb = min(256, t), 1
        assert t % tm == 0 and tm % CHUNK == 0
        grid = (nseq, t // tm)
        x_spec = pl.BlockSpec((1, tm, d), lambda b, i: (b, i, 0))
        o_spec = pl.BlockSpec((1, tm, WIDTH), lambda b, i: (b, i, 0))
    else:
        nb = min(8, nseq)
        tm = nb * CHUNK
        assert t == CHUNK and nseq % nb == 0
        grid = (nseq // nb, 1)
        x_spec = pl.BlockSpec((nb, CHUNK, d), lambda b, i: (b, 0, 0))
        o_spec = pl.BlockSpec((nb, CHUNK, WIDTH), lambda b, i: (b, 0, 0))
    ones_bd = jnp.asarray(np.kron(np.eye(HEADS, dtype=np.float32),
                                  np.ones((HEAD_DIM, HEAD_DIM), np.float32)), BF16)
    consts = (ln_g, ln_b, wr, mu, w0, w2p, a0, a2p, g2, k_k, k_a, r_k, gn_g, gn_b, ones_bd)
    state2d = state_in.reshape(nseq, HEADS * HEAD_DIM, HEAD_DIM)
    shift3d = shift_in.reshape(nseq, 1, RWKV_PROJ)
    seq_spec = lambda shape: pl.BlockSpec((nb,) + shape, lambda b, i: (b, 0, 0))
    o, shift_out, state_out = pl.pallas_call(
        functools.partial(_rwkv_kernel, tm=tm, t_real=t_real, chained=chained),
        out_shape=(jax.ShapeDtypeStruct((nseq, t, WIDTH), BF16),
                   jax.ShapeDtypeStruct((nseq, 1, RWKV_PROJ), F32),
                   jax.ShapeDtypeStruct((nseq, HEADS * HEAD_DIM, HEAD_DIM), F32)),
        grid=grid,
        in_specs=[x_spec] + [_const_spec(c.shape) for c in consts]
                 + [seq_spec((1, RWKV_PROJ)), seq_spec((HEADS * HEAD_DIM, HEAD_DIM))],
        out_specs=(o_spec, seq_spec((1, RWKV_PROJ)), seq_spec((HEADS * HEAD_DIM, HEAD_DIM))),
        scratch_shapes=[pltpu.VMEM((N_GROUPS, GROUP_W, GROUP_W), F32),
                        pltpu.VMEM((1, RWKV_PROJ), F32)],
        compiler_params=_params(("arbitrary", "arbitrary")),
        name="rwkv",
    )(x3d, *consts, shift3d, state2d)
    return (o, shift_out.reshape(nseq, RWKV_PROJ),
            state_out.reshape(nseq, HEADS, HEAD_DIM, HEAD_DIM))


def _softmax_step(carry, s, v_blk):
    m, l, acc = carry
    m_new = jnp.maximum(m, jnp.max(s, axis=-1, keepdims=True))
    alpha = jnp.exp(m - m_new)
    p = jnp.exp(s - m_new)
    l = alpha * l + jnp.sum(p, axis=-1, keepdims=True)
    acc = alpha * acc + _dot(p.astype(BF16), v_blk)
    return m_new, l, acc


def _fox_prompt_kernel(q_ref, cq_ref, k_ref, v_ref, ck_ref, o_ref, qa_ref, m_ref, acc_ref, *, tq):
    i = pl.program_id(1)
    lane = lax.broadcasted_iota(jnp.int32, (tq, LANES), 1)
    cq = cq_ref[0]
    zero = jnp.zeros((), BF16)
    for h in range(HEADS):
        q_pair = q_ref[0, :, (h // 2) * LANES:(h // 2 + 1) * LANES]
        q_h = jnp.where(lax.shift_right_logical(lane, 6) == (h % 2), q_pair, zero)
        c_h = jnp.where(lax.shift_right_logical(lane, 3) == h, cq, zero)
        qa_ref[h] = jnp.concatenate([q_h, c_h], axis=1)
    m_ref[...] = jnp.full(m_ref.shape, -jnp.inf, F32)
    acc_ref[...] = jnp.zeros(acc_ref.shape, F32)
    ones = jnp.ones((tq, LANES), BF16)

    def block(j, mask):
        rs = pl.ds(pl.multiple_of(j * tq, tq), tq)
        ck = ck_ref[0, rs, :]
        for pair in range(HEADS // 2):
            ls = slice(pair * LANES, (pair + 1) * LANES)
            ka = jnp.concatenate([k_ref[0, rs, ls], ck], axis=1)
            v_aug = jnp.concatenate([v_ref[0, rs, ls], ones], axis=1)
            for h in (2 * pair, 2 * pair + 1):
                s = _dot_nt(qa_ref[h], ka)
                if mask is not None:
                    s = jnp.where(mask, s, -jnp.inf)
                parts = [s[:, t * LANES:(t + 1) * LANES] for t in range(tq // LANES)]
                m_old = m_ref[h]
                m_new = jnp.maximum(m_old, jnp.max(functools.reduce(jnp.maximum, parts), axis=-1, keepdims=True))
                alpha = jnp.exp(m_old - m_new)
                p = jnp.concatenate([jnp.exp(t - m_new) for t in parts], axis=1).astype(BF16)
                m_ref[h] = m_new
                acc_ref[h] = jnp.concatenate([alpha, alpha], axis=1) * acc_ref[h] + _dot(p, v_aug)

    def body(j, carry):
        block(j, None)
        return carry

    lax.fori_loop(0, i, body, 0)
    row = lax.broadcasted_iota(jnp.int32, (tq, tq), 0)
    col = lax.broadcasted_iota(jnp.int32, (tq, tq), 1)
    block(i, col <= row)
    for pair in range(HEADS // 2):
        lo = acc_ref[2 * pair, :, 0:LANES] / acc_ref[2 * pair, :, LANES:2 * LANES]
        hi = acc_ref[2 * pair + 1, :, 0:LANES] / acc_ref[2 * pair + 1, :, LANES:2 * LANES]
        o_ref[0, :, pair * LANES:(pair + 1) * LANES] = jnp.where(
            lax.shift_right_logical(lane, 6) == 0, lo, hi).astype(BF16)


def _fox_prompt(q, kb, vb, cq, ck):
    b, s, _ = q.shape
    tq = min(256, s)
    assert s % tq == 0
    blk = lambda w: pl.BlockSpec((1, tq, w), lambda bi, i: (bi, i, 0))
    full = lambda w: pl.BlockSpec((1, s, w), lambda bi, i: (bi, 0, 0))
    return pl.pallas_call(
        functools.partial(_fox_prompt_kernel, tq=tq),
        out_shape=jax.ShapeDtypeStruct((b, s, WIDTH), BF16),
        grid=(b, s // tq),
        in_specs=[blk(WIDTH), blk(LANES), full(WIDTH), full(WIDTH), full(LANES)],
        out_specs=blk(WIDTH),
        scratch_shapes=[pltpu.VMEM((HEADS, tq, 2 * LANES), BF16),
                        pltpu.VMEM((HEADS, tq, LANES), F32),
                        pltpu.VMEM((HEADS, tq, 2 * LANES), F32)],
        compiler_params=_params(("arbitrary", "arbitrary")),
        name="fox_prompt",
    )(q, cq, kb, vb, ck)


def _fox_sample_kernel(pt_ref, q_ref, kn_ref, vn_ref, ckn_ref, *refs, t_new, pps, page):
    k_refs = refs[0:pps]
    v_refs = refs[pps:2 * pps]
    lf_refs = refs[2 * pps:3 * pps]
    o_ref = refs[3 * pps]
    qrow_ref, m_ref, l_ref, acc_ref, run_ref, pad_ref = refs[3 * pps + 1:]
    del pt_ref
    step = pl.program_id(1)
    n_rows = t_new * HEADS
    rowh = lax.broadcasted_iota(jnp.int32, (n_rows, WIDTH), 0) & (HEADS - 1)
    laneh = lax.shift_right_logical(lax.broadcasted_iota(jnp.int32, (n_rows, WIDTH), 1), 6)
    head_mask = rowh == laneh

    @pl.when(step == 0)
    def _():
        q = q_ref[0]
        rows = jnp.concatenate([jnp.broadcast_to(q[t:t + 1, :], (HEADS, WIDTH)) for t in range(t_new)],
                               axis=0)
        qrow_ref[...] = jnp.where(head_mask, rows, 0.0).astype(BF16)
        m_ref[...] = jnp.full(m_ref.shape, -jnp.inf, F32)
        l_ref[...] = jnp.zeros(l_ref.shape, F32)
        acc_ref[...] = jnp.zeros(acc_ref.shape, F32)
        run_ref[...] = jnp.zeros(run_ref.shape, F32)

    qrows = qrow_ref[...]
    lf = jnp.concatenate([r[0] for r in lf_refs], axis=0)
    newer = lax.broadcasted_iota(jnp.int32, (page, page), 0) > lax.broadcasted_iota(jnp.int32, (page, page), 1)
    suf = _dot3_rhs(lf, jnp.where(newer, 1.0, 0.0).astype(BF16))
    run = run_ref[...]
    scores = []
    for n in range(pps):
        blk = slice(n * HEADS, (n + 1) * HEADS)
        bias = suf[blk] + run
        run = run + (suf[blk][:, 0:1] + lf[blk][:, 0:1])
        scores.append(_dot(qrows, k_refs[n][0].astype(BF16)) + jnp.concatenate([bias] * t_new, axis=0))
    run_ref[...] = run
    m_old = m_ref[...]
    m_new = m_old
    for s in scores:
        m_new = jnp.maximum(m_new, jnp.max(s, axis=-1, keepdims=True))
    alpha = jnp.exp(m_old - m_new)
    l_new = alpha * l_ref[...]
    pv = None
    for n in range(pps):
        p = jnp.exp(scores[n] - m_new)
        l_new = l_new + jnp.sum(p, axis=-1, keepdims=True)
        part = _dot_nt(p.astype(BF16), v_refs[n][0].astype(BF16))
        pv = part if pv is None else pv + part
    m_ref[...] = m_new
    l_ref[...] = l_new
    acc_ref[...] = alpha * acc_ref[...] + pv

    @pl.when(step == pl.num_programs(1) - 1)
    def _():
        pad_ref[...] = jnp.zeros(pad_ref.shape, F32)
        pad_ref[0, 0:t_new, :] = kn_ref[0]
        pad_ref[1, 0:t_new, :] = vn_ref[0]
        pad_ref[2, 0:t_new, 0:LANES] = ckn_ref[0]
        kn = pad_ref[0].astype(BF16)
        vn = pad_ref[1].astype(BF16)
        ckn = pad_ref[2, :, 0:LANES].astype(BF16)
        r8 = lax.broadcasted_iota(jnp.int32, (n_rows, LANES), 0) & (HEADS - 1)
        l8 = lax.broadcasted_iota(jnp.int32, (n_rows, LANES), 1)
        sel = (lax.shift_right_logical(l8, 3) == r8) & ((l8 & (BIAS_STRIDE - 1)) >= BIAS_PARTS) \
            & ((l8 & (BIAS_STRIDE - 1)) < 2 * BIAS_PARTS)
        cqr = jnp.where(sel, 1.0, 0.0).astype(BF16)
        s = _dot_nt(jnp.concatenate([qrows, cqr], axis=1), jnp.concatenate([kn, ckn], axis=1))
        key = lax.broadcasted_iota(jnp.int32, (n_rows, page), 1)
        tok = lax.shift_right_logical(lax.broadcasted_iota(jnp.int32, (n_rows, page), 0), 3)
        s = jnp.where(key <= tok, s, -jnp.inf)
        _, l, acc = _softmax_step((m_ref[...], l_ref[...], acc_ref[...]), s, vn)
        o = jnp.where(head_mask, acc / l, 0.0)
        o_ref[0] = jnp.sum(o.reshape(t_new, HEADS, WIDTH), axis=1).astype(BF16)


def _fox_sample(q, kb, vb, ck, cache_k, cache_v, cache_lf_t, page_table):
    db, t_new, _ = q.shape
    n_pages = page_table.shape[1]
    n_pool, _, page = cache_k.shape
    pps = 16
    assert n_pages % pps == 0 and page == LANES and t_new <= HEADS
    n_rows = t_new * HEADS
    pt = page_table.reshape(-1)

    def page_idx(n):
        return lambda b, s, pt_ref: (pt_ref[b * n_pages + n_pages - 1 - (s * pps + n)], 0, 0)

    new = lambda w: pl.BlockSpec((1, t_new, w), lambda b, s, pt_ref: (b, 0, 0))
    in_specs = ([new(WIDTH), new(WIDTH), new(WIDTH), new(LANES)]
                + [pl.BlockSpec((1, WIDTH, page), page_idx(n)) for n in range(pps)]
                + [pl.BlockSpec((1, WIDTH, page), page_idx(n)) for n in range(pps)]
                + [pl.BlockSpec((1, HEADS, page), page_idx(n)) for n in range(pps)])
    grid_spec = pltpu.PrefetchScalarGridSpec(
        num_scalar_prefetch=1,
        grid=(db, n_pages // pps),
        in_specs=in_specs,
        out_specs=new(WIDTH),
        scratch_shapes=[pltpu.VMEM((n_rows, WIDTH), BF16),
                        pltpu.VMEM((n_rows, 1), F32), pltpu.VMEM((n_rows, 1), F32),
                        pltpu.VMEM((n_rows, WIDTH), F32), pltpu.VMEM((HEADS, 1), F32),
                        pltpu.VMEM((3, page, WIDTH), F32)],
    )
    return pl.pallas_call(
        functools.partial(_fox_sample_kernel, t_new=t_new, pps=pps, page=page),
        out_shape=jax.ShapeDtypeStruct((db, t_new, WIDTH), BF16),
        grid_spec=grid_spec,
        compiler_params=_params(("arbitrary", "arbitrary")),
        name="fox_sample",
    )(pt, q, kb, vb, ck, *([cache_k] * pps), *([cache_v] * pps), *([cache_lf_t] * pps))


def _mlp_kernel(x_ref, or_ref, of_ref, gt_ref, lg_ref, lb_ref, wb_ref, wo_ref, g1_ref, b1_ref,
                wu_ref, wd_ref, g2_ref, b2_ref, y_ref, *, alpha, ff_chunk):
    d = x_ref.shape[1]
    hp = _layer_norm(x_ref[...], lg_ref[...], lb_ref[...])
    gt = gt_ref[...]
    merged = (gt[:, :d].astype(F32) * _dot(or_ref[...], wb_ref[0:WIDTH, :])
              + gt[:, d:].astype(F32) * _dot(of_ref[...], wb_ref[WIDTH:2 * WIDTH, :]))
    x1 = _layer_norm(alpha * hp + _dot(merged.astype(BF16), wo_ref[...]), g1_ref[...], b1_ref[...])
    x1b = x1.astype(BF16)
    h = jnp.zeros_like(x1)
    for c in range(wu_ref.shape[1] // ff_chunk):
        cs = slice(c * ff_chunk, (c + 1) * ff_chunk)
        up = jnp.maximum(_dot(x1b, wu_ref[:, cs]), 0.0)
        h = h + _dot((up * up).astype(BF16), wd_ref[cs, :])
    y_ref[...] = _layer_norm(alpha * x1 + h, g2_ref[...], b2_ref[...])


def _mlp(x2d, o_r, o_f, gates, ln_g, ln_b, wb, wo, g1, b1, wu, wd, g2, b2, alpha):
    m, d = x2d.shape
    tm = min(512, m)
    assert m % tm == 0
    row = lambda w: pl.BlockSpec((tm, w), lambda i: (i, 0))
    consts = (ln_g, ln_b, wb, wo, g1, b1, wu, wd, g2, b2)
    return pl.pallas_call(
        functools.partial(_mlp_kernel, alpha=alpha, ff_chunk=1024),
        out_shape=jax.ShapeDtypeStruct((m, d), F32),
        grid=(m // tm,),
        in_specs=[row(d), row(WIDTH), row(WIDTH), row(2 * d)] + [_const_spec(c.shape) for c in consts],
        out_specs=row(d),
        compiler_params=_params(("arbitrary",)),
        name="mlp",
    )(x2d, o_r, o_f, gates, *consts)


def kernel(x_prompt, x_sample, state_wkv, state_shift, cache_k, cache_v, cache_logf, page_table, ln_in_g, ln_in_b, w_in, rwkv_mu, rwkv_w0, rwkv_w2, rwkv_a0, rwkv_a2, rwkv_g2, rwkv_k_k, rwkv_k_a, rwkv_r_k, rwkv_gn_g, rwkv_gn_b, fox_b_f, w_branch, w_out, ln1_g, ln1_b, w_up, w_down, ln2_g, ln2_b):
    depth = w_in.shape[0]
    assert depth == 1, "the entry LayerNorm is fused into the first layer's kernels"
    bp, seq, d = x_prompt.shape
    db, dseq, _ = x_sample.shape
    n_pool, page = cache_k.shape[1], cache_k.shape[2]
    alpha = (2.0 * depth) ** 0.25
    row2 = lambda t: t.reshape(1, -1).astype(F32)
    ln_g, ln_b = row2(ln_in_g), row2(ln_in_b)

    l = 0
    w = w_in[l]
    fox_off = RWKV_PROJ
    wr = w[:, :RWKV_PROJ].astype(BF16)
    wq = w[:, fox_off:fox_off + WIDTH].astype(BF16)
    wk = w[:, fox_off + WIDTH:fox_off + 2 * WIDTH].astype(BF16)
    wv = w[:, fox_off + 2 * WIDTH:fox_off + 3 * WIDTH].astype(BF16)
    wf = jnp.pad(w[:, fox_off + 3 * WIDTH:fox_off + 3 * WIDTH + HEADS], ((0, 0), (0, LANES - HEADS))).astype(BF16)
    wg = w[:, fox_off + 3 * WIDTH + HEADS:].astype(BF16)
    bf = jnp.pad(fox_b_f[l].reshape(1, HEADS), ((0, 0), (0, LANES - HEADS))).astype(F32)
    zl = jnp.zeros((DECAY_LORA, WIDTH), BF16)
    w2p = jnp.concatenate([rwkv_w2[l].astype(BF16), zl], axis=0)
    a2p = jnp.concatenate([zl, rwkv_a2[l].astype(BF16)], axis=0)
    rwkv_consts = (ln_g, ln_b, wr, row2(rwkv_mu[l]), row2(rwkv_w0[l]), w2p, row2(rwkv_a0[l]), a2p,
                   rwkv_g2[l].astype(BF16), row2(rwkv_k_k[l]), row2(rwkv_k_a[l]), row2(rwkv_r_k[l]),
                   row2(rwkv_gn_g[l]), row2(rwkv_gn_b[l]))
    mlp_consts = (ln_g, ln_b, w_branch[l].astype(BF16), w_out[l].astype(BF16), row2(ln1_g[l]),
                  row2(ln1_b[l]), w_up[l].astype(BF16), w_down[l].astype(BF16), row2(ln2_g[l]),
                  row2(ln2_b[l]))

    xp2 = x_prompt.reshape(bp * seq, d)
    q, k32, v32, kb, vb, lf, cq, ck, gates = _proj(xp2, ln_g, ln_b, wq, wk, wv, wf, bf, wg, seq)
    to_seq = lambda t: t.reshape(bp, seq, t.shape[-1])
    o_f = _fox_prompt(to_seq(q), to_seq(kb), to_seq(vb), to_seq(cq), to_seq(ck))
    o_r, p_shift, p_wkv = _rwkv(x_prompt, seq, *rwkv_consts,
                                jnp.zeros((bp, RWKV_PROJ), F32),
                                jnp.zeros((bp, HEADS, HEAD_DIM, HEAD_DIM), F32))
    y_prompt = _mlp(xp2, o_r.reshape(bp * seq, WIDTH), o_f.reshape(bp * seq, WIDTH), gates,
                    *mlp_consts, alpha).reshape(bp, seq, d)
    p_k = k32.reshape(1, bp, seq, HEADS, HEAD_DIM)
    p_v = v32.reshape(1, bp, seq, HEADS, HEAD_DIM)
    p_lf = lf.reshape(1, bp, seq, HEADS)

    xs2 = x_sample.reshape(db * dseq, d)
    q, k32, v32, kb, vb, lf, cq, ck, gates = _proj(xs2, ln_g, ln_b, wq, wk, wv, wf, bf, wg, dseq)
    to_seq = lambda t: t.reshape(db, dseq, t.shape[-1])
    o_f = _fox_sample(to_seq(q.astype(F32)), to_seq(k32), to_seq(v32), to_seq(ck.astype(F32)),
                      jnp.transpose(cache_k[l], (0, 2, 3, 1)).reshape(n_pool, WIDTH, page),
                      jnp.transpose(cache_v[l], (0, 2, 3, 1)).reshape(n_pool, WIDTH, page),
                      jnp.swapaxes(cache_logf[l], 1, 2), page_table)
    xs_pad = jnp.pad(x_sample, ((0, 0), (0, CHUNK - dseq), (0, 0)))
    o_r, s_shift, s_wkv = _rwkv(xs_pad, dseq, *rwkv_consts, state_shift[l], state_wkv[l])
    y_sample = _mlp(xs2, o_r[:, :dseq].reshape(db * dseq, WIDTH), o_f.reshape(db * dseq, WIDTH), gates,
                    *mlp_consts, alpha).reshape(db, dseq, d)
    s_k = k32.reshape(1, db, dseq, HEADS, HEAD_DIM)
    s_v = v32.reshape(1, db, dseq, HEADS, HEAD_DIM)
    s_lf = lf.reshape(1, db, dseq, HEADS)

    return (y_prompt, y_sample, p_wkv[None], p_shift[None], p_k, p_v, p_lf,
            s_wkv[None], s_shift[None], s_k, s_v, s_lf)
```

```python
import functools
import math

import numpy as np
import jax
import jax.numpy as jnp
from jax import lax
from jax.experimental import pallas as pl
from jax.experimental.pallas import tpu as pltpu

F32 = jnp.float32
BF16 = jnp.bfloat16

HEADS = 8
HEAD_DIM = 64
WIDTH = HEADS * HEAD_DIM
DECAY_LORA = 64
AAA_LORA = 64
GATE_LORA = 128
RWKV_PROJ = 3 * WIDTH + DECAY_LORA + AAA_LORA + GATE_LORA
LORA_OFF = 3 * WIDTH
GATE_LORA_OFF = LORA_OFF + DECAY_LORA + AAA_LORA
GN_EPS = 64e-5
LN_EPS = 1e-5
QK_SCALE = HEAD_DIM ** -0.5
LOG2_E = math.log2(math.e)

LANES = 128
KV_BLOCK = 256
CHUNK = 64
GROUP_HEADS = 4
GROUP_W = GROUP_HEADS * HEAD_DIM
N_GROUPS = HEADS // GROUP_HEADS
BIAS_PARTS = 3
BIAS_STRIDE = 8
VMEM_LIMIT = 56 * 1024 * 1024


def _dot(a, b):
    return jnp.dot(a, b, preferred_element_type=F32)


def _dot_nt(a, b):
    return lax.dot_general(a, b, (((1,), (1,)), ((), ())), preferred_element_type=F32)


def _dot_tn(a, b):
    return lax.dot_general(a, b, (((0,), (0,)), ((), ())), preferred_element_type=F32)


def _layer_norm(x, g, b):
    mu = jnp.mean(x, -1, keepdims=True)
    xc = x - mu
    var = jnp.mean(xc * xc, -1, keepdims=True)
    return xc * lax.rsqrt(var + LN_EPS) * g + b


def _split3(x):
    p1 = x.astype(BF16)
    r1 = x - p1.astype(F32)
    p2 = r1.astype(BF16)
    p3 = (r1 - p2.astype(F32)).astype(BF16)
    return p1, p2, p3


def _dot3(m_bf16, x):
    p1, p2, p3 = _split3(x)
    return _dot(m_bf16, p1) + _dot(m_bf16, p2) + _dot(m_bf16, p3)


def _dot3_rhs(x, m_bf16):
    p1, p2, p3 = _split3(x)
    return _dot(p1, m_bf16) + _dot(p2, m_bf16) + _dot(p3, m_bf16)


def _seq_tri(n, seq_len, upper=False):
    row = lax.broadcasted_iota(jnp.int32, (n, n), 0)
    col = lax.broadcasted_iota(jnp.int32, (n, n), 1)
    keep = (row <= col) if upper else (col <= row)
    if seq_len < n:
        sh = int(math.log2(seq_len))
        assert 1 << sh == seq_len
        keep = keep & (lax.shift_right_logical(row, sh) == lax.shift_right_logical(col, sh))
    return jnp.where(keep, 1.0, 0.0).astype(BF16)


def _const_spec(shape):
    nd = len(shape)
    return pl.BlockSpec(shape, lambda *_: (0,) * nd, pipeline_mode=pl.Buffered(1))


def _params(sem):
    return pltpu.CompilerParams(dimension_semantics=sem, vmem_limit_bytes=VMEM_LIMIT)


def _bias_consts():
    e = np.zeros((2 * BIAS_PARTS, LANES, LANES), np.float32)
    ones_q = np.zeros((1, LANES), np.float32)
    ones_k = np.zeros((1, LANES), np.float32)
    for h in range(HEADS):
        for j in range(BIAS_PARTS):
            e[j, h, BIAS_STRIDE * h + j] = 1.0
            e[BIAS_PARTS + j, h, BIAS_STRIDE * h + BIAS_PARTS + j] = 1.0
            ones_q[0, BIAS_STRIDE * h + BIAS_PARTS + j] = 1.0
            ones_k[0, BIAS_STRIDE * h + j] = 1.0
    return jnp.asarray(e, BF16), jnp.asarray(ones_q), jnp.asarray(ones_k)


def _proj_kernel(x_ref, g_ref, b_ref, wq_ref, wkt_ref, wvt_ref, wft_ref, bf_ref, wg_ref,
                 e_ref, et_ref, oq_ref, ok_ref,
                 q_ref, kt_ref, vt_ref, ktb_ref, vtb_ref, lft_ref, cq_ref, ckt_ref, gt_ref,
                 carry_ref, *, seq_len, tm, logit_scale):
    xn = _layer_norm(x_ref[...], g_ref[...], b_ref[...]).astype(BF16)
    q_ref[...] = (_dot(xn, wq_ref[...]) * (QK_SCALE * logit_scale)).astype(BF16)
    gt_ref[...] = jax.nn.sigmoid(_dot(xn, wg_ref[...])).astype(BF16)
    for w_ref, o32_ref, ob_ref in ((wkt_ref, kt_ref, ktb_ref), (wvt_ref, vt_ref, vtb_ref)):
        t = _dot_nt(w_ref[...], xn)
        o32_ref[0] = t
        for jb in range(tm // KV_BLOCK):
            ob_ref[jb] = t[:, jb * KV_BLOCK:(jb + 1) * KV_BLOCK].astype(BF16)

    rowi = lax.broadcasted_iota(jnp.int32, (LANES, tm), 0)
    lft = jax.nn.log_sigmoid(_dot_nt(wft_ref[...], xn) + bf_ref[...])
    lft = jnp.where(rowi < HEADS, lft, 0.0)
    lft_ref[0] = lft[:HEADS, :]

    ct = _dot3_rhs(lft, _seq_tri(tm, seq_len, upper=True))
    if seq_len > tm:
        @pl.when(pl.program_id(0) % (seq_len // tm) == 0)
        def _():
            carry_ref[...] = jnp.zeros_like(carry_ref)
        ct = ct + carry_ref[...]
        carry_ref[...] = ct[:, tm - 1:tm]

    if logit_scale != 1.0:
        ct = ct * logit_scale
    c1t, c2t, c3t = _split3(ct)
    ckt = ok_ref[...] - (_dot(et_ref[0], c1t) + _dot(et_ref[1], c2t) + _dot(et_ref[2], c3t))
    for jb in range(tm // KV_BLOCK):
        ckt_ref[jb] = ckt[:, jb * KV_BLOCK:(jb + 1) * KV_BLOCK].astype(BF16)
    c1, c2, c3 = _split3(ct.T)
    cq = _dot(c1, e_ref[0]) + _dot(c2, e_ref[1]) + _dot(c3, e_ref[2]) + oq_ref[...]
    cq_ref[...] = cq.astype(BF16)


def _proj(x2d, ln_g, ln_b, wq, wkt, wvt, wft, bf_col, wg, seq_len, logit_scale):
    m, d = x2d.shape
    tm = min(512, m)
    assert m % tm == 0 and (seq_len % tm == 0 or tm % seq_len == 0) and tm % KV_BLOCK == 0
    e, ones_q, ones_k = _bias_consts()
    e_q = e[:BIAS_PARTS]
    e_kt = jnp.swapaxes(e[BIAS_PARTS:], 1, 2)
    ones_k_col = ones_k.reshape(LANES, 1)
    if seq_len >= tm:
        tps = seq_len // tm
        groups, cols = m // seq_len, seq_len
        tr = lambda r: pl.BlockSpec((1, r, tm), lambda i: (i // tps, 0, i % tps))
    else:
        groups, cols = m // tm, tm
        tr = lambda r: pl.BlockSpec((1, r, tm), lambda i: (i, 0, 0))
    nkb = tm // KV_BLOCK
    trb = lambda r: pl.BlockSpec((nkb, r, KV_BLOCK), lambda i: (i, 0, 0))
    row = lambda w: pl.BlockSpec((tm, w), lambda i: (i, 0))
    out_shape = (
        jax.ShapeDtypeStruct((m, WIDTH), BF16),
        jax.ShapeDtypeStruct((groups, WIDTH, cols), F32),
        jax.ShapeDtypeStruct((groups, WIDTH, cols), F32),
        jax.ShapeDtypeStruct((m // KV_BLOCK, WIDTH, KV_BLOCK), BF16),
        jax.ShapeDtypeStruct((m // KV_BLOCK, WIDTH, KV_BLOCK), BF16),
        jax.ShapeDtypeStruct((groups, HEADS, cols), F32),
        jax.ShapeDtypeStruct((m, LANES), BF16),
        jax.ShapeDtypeStruct((m // KV_BLOCK, LANES, KV_BLOCK), BF16),
        jax.ShapeDtypeStruct((m, 2 * d), BF16),
    )
    consts = (ln_g, ln_b, wq, wkt, wvt, wft, bf_col, wg, e_q, e_kt, ones_q, ones_k_col)
    return pl.pallas_call(
        functools.partial(_proj_kernel, seq_len=seq_len, tm=tm, logit_scale=logit_scale),
        out_shape=out_shape,
        grid=(m // tm,),
        in_specs=[row(d)] + [_const_spec(c.shape) for c in consts],
        out_specs=(row(WIDTH), tr(WIDTH), tr(WIDTH), trb(WIDTH), trb(WIDTH), tr(HEADS),
                   row(LANES), trb(LANES), row(2 * d)),
        scratch_shapes=[pltpu.VMEM((LANES, 1), F32)],
        compiler_params=_params(("arbitrary",)),
        name="proj",
    )(x2d, *consts)


def _group_mask(rows):
    r = lax.broadcasted_iota(jnp.int32, (rows, GROUP_W), 0)
    c = lax.broadcasted_iota(jnp.int32, (rows, GROUP_W), 1)
    sh = int(math.log2(HEAD_DIM))
    return (lax.shift_right_logical(r, sh) & (GROUP_HEADS - 1)) == lax.shift_right_logical(c, sh)


def _bd(y, mask):
    yb = y.astype(BF16)
    return jnp.where(mask, jnp.concatenate([yb] * GROUP_HEADS, axis=0), jnp.zeros((), BF16))


def _state_to_bd(st, mask_bd):
    out = []
    for g in range(N_GROUPS):
        blk = st[g * GROUP_W:(g + 1) * GROUP_W, :]
        out.append(jnp.where(mask_bd, jnp.concatenate([blk] * GROUP_HEADS, axis=1), 0.0))
    return out


def _state_from_bd(state, mask_bd):
    rows = []
    for sg in state:
        sm = jnp.where(mask_bd, sg, 0.0)
        acc = sm[:, 0:HEAD_DIM]
        for h in range(1, GROUP_HEADS):
            acc = acc + sm[:, h * HEAD_DIM:(h + 1) * HEAD_DIM]
        rows.append(acc)
    return jnp.concatenate(rows, axis=0)


def _rwkv_kernel(x_ref, g_ref, b_ref, wr_ref, mu_ref, w0_ref, w2_ref, a0_ref, a2_ref, g2_ref,
                 kk_ref, ka_ref, rk_ref, gng_ref, gnb_ref, ones_ref, shift_ref, state_ref,
                 o_ref, shift_out_ref, state_out_ref,
                 s_ref, prev_ref, *, tm, t_real, chained):
    nc = tm // CHUNK
    mask_bd = _group_mask(GROUP_W)
    rowi = lax.broadcasted_iota(jnp.int32, (tm, 1), 0)

    if chained:
        @pl.when(pl.program_id(1) == 0)
        def _():
            prev_ref[...] = shift_ref[0]
            for g, sg in enumerate(_state_to_bd(state_ref[0], mask_bd)):
                s_ref[g] = sg

    xn = _layer_norm(x_ref[...].reshape(tm, x_ref.shape[-1]), g_ref[...], b_ref[...]).astype(BF16)
    p = _dot(xn, wr_ref[...])
    if chained:
        p_prev = jnp.where(rowi == 0, prev_ref[...], pltpu.roll(p, 1, 0))
        prev_ref[...] = p[tm - 1:tm, :]
        shift_out_ref[0] = p[tm - 1:tm, :]
    else:
        first = jnp.concatenate([jnp.broadcast_to(shift_ref[c], (CHUNK, RWKV_PROJ)) for c in range(nc)], axis=0)
        p_prev = jnp.where((rowi & (CHUNK - 1)) == 0, first, pltpu.roll(p, 1, 0))
        for c in range(nc):
            shift_out_ref[c] = p[c * CHUNK + t_real - 1:c * CHUNK + t_real, :]
    pm = p + (p_prev - p) * mu_ref[...]

    r = pm[:, 0:WIDTH]
    k = pm[:, WIDTH:2 * WIDTH]
    v = pm[:, 2 * WIDTH:3 * WIDTH]
    dwa = pm[:, LORA_OFF:GATE_LORA_OFF]
    dg = pm[:, GATE_LORA_OFF:RWKV_PROJ]
    w_log = -jax.nn.softplus(-(w0_ref[...] + _dot(jnp.tanh(dwa).astype(BF16), w2_ref[...]))) - 0.5
    lw = -jnp.exp(w_log)
    a = jax.nn.sigmoid(a0_ref[...] + _dot(dwa.astype(BF16), a2_ref[...]))
    gate = _dot(jax.nn.sigmoid(dg).astype(BF16), g2_ref[...])
    kk = k * kk_ref[...]
    ss = _dot((kk * kk).astype(BF16), ones_ref[...])
    kk = kk * lax.rsqrt(jnp.maximum(ss, 1e-24))
    k = k * (1.0 + (a - 1.0) * ka_ref[...])
    bonus = _dot((r * k * rk_ref[...]).astype(BF16), ones_ref[...]) * v
    kka = kk * a
    if not chained and t_real < CHUNK:
        live = (rowi & (CHUNK - 1)) < t_real
        lw = jnp.where(live, lw, 0.0)
        r, k, v, kk, kka = (jnp.where(live, t, 0.0) for t in (r, k, v, kk, kka))

    tri3 = jnp.concatenate([_seq_tri(CHUNK, CHUNK)] * BIAS_PARTS, axis=1)
    lw_parts = _split3(lw)
    cl = jnp.concatenate(
        [_dot(tri3, jnp.concatenate([t[c * CHUNK:(c + 1) * CHUNK, :] for t in lw_parts], axis=0))
         for c in range(nc)], axis=0) if nc > 1 else _dot(tri3, jnp.concatenate(lw_parts, axis=0))
    e_in = jnp.exp(cl)
    e_ex = jnp.exp(cl - lw)
    e_ng = jnp.exp(-cl)
    rt = r * e_in
    at = -kk * e_ex
    bt = kka * e_ng
    kt = k * e_ng

    row = lax.broadcasted_iota(jnp.int32, (CHUNK, GROUP_W), 0)
    col = lax.broadcasted_iota(jnp.int32, (CHUNK, GROUP_W), 1) & (CHUNK - 1)
    strict = col < row
    incl = col <= row
    blk16 = lax.shift_right_logical(row, 4) == lax.shift_right_logical(col, 4)
    blk32 = lax.shift_right_logical(row, 5) == lax.shift_right_logical(col, 5)

    def mm(x, y):
        return _dot(x.astype(BF16), _bd(y, mask_bd))

    probs = [(c, g) for c in range(nc) for g in range(N_GROUPS)]
    sl = lambda x, c, g: x[c * CHUNK:(c + 1) * CHUNK, g * GROUP_W:(g + 1) * GROUP_W]
    each = lambda f, *ls: [f(*xs) for xs in zip(*ls)]
    rt_, at_, bt_, kt_, v_ = ([sl(x, c, g) for c, g in probs] for x in (rt, at, bt, kt, v))
    cl_end = [cl[(c + 1) * CHUNK - 1:(c + 1) * CHUNK, :] for c in range(nc)]
    g_end = [jnp.exp(ce) for ce in cl_end]
    e_end = [jnp.exp(cl_end[c] - cl[c * CHUNK:(c + 1) * CHUNK, :]) for c in range(nc)]
    bh_ = [sl(kka, c, g) * e_end[c][:, g * GROUP_W:(g + 1) * GROUP_W] for c, g in probs]
    kh_ = [sl(k, c, g) * e_end[c][:, g * GROUP_W:(g + 1) * GROUP_W] for c, g in probs]

    ar_ = each(lambda a_, r_: jnp.concatenate([a_, r_], axis=0).astype(BF16), at_, rt_)
    gb_ = each(lambda ar, b_: _dot_nt(ar, _bd(b_, mask_bd)), ar_, bt_)
    gk_ = each(lambda ar, k_: _dot_nt(ar, _bd(k_, mask_bd)), ar_, kt_)
    n_ab = [jnp.where(strict, x[:CHUNK], 0.0) for x in gb_]
    a_ak = [jnp.where(strict, x[:CHUNK], 0.0) for x in gk_]
    p_rb = [jnp.where(incl, x[CHUNK:], 0.0) for x in gb_]
    p_rk = [jnp.where(incl, x[CHUNK:], 0.0) for x in gk_]
    uv_ = each(mm, a_ak, v_)

    n_d = [jnp.where(blk16, x, 0.0) for x in n_ab]
    n_1 = [jnp.where(blk32 & jnp.logical_not(blk16), x, 0.0) for x in n_ab]
    n_2 = [jnp.where(blk32, 0.0, x) for x in n_ab]
    add = lambda *xs: functools.reduce(lambda a_, b_: a_ + b_, xs)
    n2 = each(mm, n_d, n_d)
    n3 = each(mm, n2, n_d)
    n4 = each(mm, n2, n2)
    q4 = each(add, n_d, n2, n3)
    q8 = each(add, q4, n4, each(mm, n4, q4))
    n8 = each(mm, n4, n4)
    t16 = each(add, q8, n8, each(mm, n8, q8))
    x1 = each(add, n_1, each(mm, n_1, t16))
    t32 = each(add, t16, x1, each(mm, t16, x1))
    x2 = each(add, n_2, each(mm, n_2, t32))
    t_ = each(add, t32, x2, each(mm, t32, x2))

    wt_ = each(add, at_, each(mm, t_, at_))
    ut_ = each(add, uv_, each(mm, t_, uv_))
    mw_ = each(lambda w_, b_: jnp.where(mask_bd, _dot_tn(w_.astype(BF16), b_.astype(BF16)), 0.0).astype(BF16),
               wt_, bh_)
    dd_ = each(lambda u_, vv, b_, k_: jnp.where(mask_bd, _dot_tn(
        jnp.concatenate([u_, vv], axis=0).astype(BF16), jnp.concatenate([b_, k_], axis=0).astype(BF16)), 0.0),
               ut_, v_, bh_, kh_)
    qq_ = each(lambda r_, x: (r_ + x).astype(BF16), rt_, each(mm, p_rb, wt_))
    o0_ = each(add, each(mm, p_rb, ut_), each(mm, p_rk, v_))

    if chained:
        state = [s_ref[g] for g in range(N_GROUPS)]
    o_chunks = []
    for c in range(nc):
        if not chained:
            state = _state_to_bd(state_ref[c], mask_bd)
        o_groups = []
        for g in range(N_GROUPS):
            j = c * N_GROUPS + g
            s0 = state[g]
            s0b = s0.astype(BF16)
            o_groups.append(_dot_nt(qq_[j], s0b) + o0_[j])
            state[g] = s0 * g_end[c][:, g * GROUP_W:(g + 1) * GROUP_W] + _dot(s0b, mw_[j]) + dd_[j]
        o_chunks.append(jnp.concatenate(o_groups, axis=1))
        if not chained:
            state_out_ref[c] = _state_from_bd(state, mask_bd)
    if chained:
        for g in range(N_GROUPS):
            s_ref[g] = state[g]
    o = jnp.concatenate(o_chunks, axis=0) if len(o_chunks) > 1 else o_chunks[0]

    inv_n = 1.0 / HEAD_DIM
    mean = _dot(o.astype(BF16), ones_ref[...]) * inv_n
    oc = o - mean
    var = _dot((oc * oc).astype(BF16), ones_ref[...]) * inv_n
    on = oc * lax.rsqrt(var + GN_EPS) * gng_ref[...] + gnb_ref[...]
    o_ref[...] = ((on + bonus) * gate).astype(BF16).reshape(o_ref.shape)

    if chained:
        @pl.when(pl.program_id(1) == pl.num_programs(1) - 1)
        def _():
            state_out_ref[0] = _state_from_bd([s_ref[g] for g in range(N_GROUPS)], mask_bd)


def _rwkv(x3d, t_real, ln_g, ln_b, wr, mu, w0, w2p, a0, a2p, g2, k_k, k_a, r_k, gn_g, gn_b,
          shift_in, state_in):
    nseq, t, d = x3d.shape
    chained = t_real == t
    if chained:
        tm, nb = min(512, t), 1
        assert t % tm == 0 and tm % CHUNK == 0
        grid = (nseq, t // tm)
        x_spec = pl.BlockSpec((1, tm, d), lambda b, i: (b, i, 0))
        o_spec = pl.BlockSpec((1, tm, WIDTH), lambda b, i: (b, i, 0))
    else:
        nb = min(8, nseq)
        tm = nb * CHUNK
        assert t == CHUNK and nseq % nb == 0
        grid = (nseq // nb, 1)
        x_spec = pl.BlockSpec((nb, CHUNK, d), lambda b, i: (b, 0, 0))
        o_spec = pl.BlockSpec((nb, CHUNK, WIDTH), lambda b, i: (b, 0, 0))
    ones_bd = jnp.asarray(np.kron(np.eye(HEADS, dtype=np.float32),
                                  np.ones((HEAD_DIM, HEAD_DIM), np.float32)), BF16)
    consts = (ln_g, ln_b, wr, mu, w0, w2p, a0, a2p, g2, k_k, k_a, r_k, gn_g, gn_b, ones_bd)
    state2d = state_in.reshape(nseq, HEADS * HEAD_DIM, HEAD_DIM)
    shift3d = shift_in.reshape(nseq, 1, RWKV_PROJ)
    seq_spec = lambda shape: pl.BlockSpec((nb,) + shape, lambda b, i: (b, 0, 0))
    o, shift_out, state_out = pl.pallas_call(
        functools.partial(_rwkv_kernel, tm=tm, t_real=t_real, chained=chained),
        out_shape=(jax.ShapeDtypeStruct((nseq, t, WIDTH), BF16),
                   jax.ShapeDtypeStruct((nseq, 1, RWKV_PROJ), F32),
                   jax.ShapeDtypeStruct((nseq, HEADS * HEAD_DIM, HEAD_DIM), F32)),
        grid=grid,
        in_specs=[x_spec] + [_const_spec(c.shape) for c in consts]
                 + [seq_spec((1, RWKV_PROJ)), seq_spec((HEADS * HEAD_DIM, HEAD_DIM))],
        out_specs=(o_spec, seq_spec((1, RWKV_PROJ)), seq_spec((HEADS * HEAD_DIM, HEAD_DIM))),
        scratch_shapes=[pltpu.VMEM((N_GROUPS, GROUP_W, GROUP_W), F32),
                        pltpu.VMEM((1, RWKV_PROJ), F32)],
        compiler_params=_params(("arbitrary", "arbitrary")),
        name="rwkv",
    )(x3d, *consts, shift3d, state2d)
    return (o, shift_out.reshape(nseq, RWKV_PROJ),
            state_out.reshape(nseq, HEADS, HEAD_DIM, HEAD_DIM))


def _softmax_step(carry, s, v_blk):
    m, l, acc = carry
    m_new = jnp.maximum(m, jnp.max(s, axis=-1, keepdims=True))
    alpha = jnp.exp(m - m_new)
    p = jnp.exp(s - m_new)
    l = alpha * l + jnp.sum(p, axis=-1, keepdims=True)
    acc = alpha * acc + _dot(p.astype(BF16), v_blk)
    return m_new, l, acc


def _fox_prompt_kernel(q_ref, cq_ref, k_ref, v_ref, ck_ref, o_ref, qa_ref, m_ref, acc_ref, *, tq):
    i = pl.program_id(1)
    lane = lax.broadcasted_iota(jnp.int32, (tq, LANES), 1)
    cq = cq_ref[0]
    zero = jnp.zeros((), BF16)
    for h in range(HEADS):
        q_pair = q_ref[0, :, (h // 2) * LANES:(h // 2 + 1) * LANES]
        q_h = jnp.where(lax.shift_right_logical(lane, 6) == (h % 2), q_pair, zero)
        c_h = jnp.where(lax.shift_right_logical(lane, 3) == h, cq, zero)
        qa_ref[h] = jnp.concatenate([q_h, c_h], axis=1)
    m_ref[...] = jnp.full(m_ref.shape, -jnp.inf, F32)
    acc_ref[...] = jnp.zeros(acc_ref.shape, F32)
    ones = jnp.ones((LANES, tq), BF16)

    def block(j, mask):
        ck = ck_ref[j]
        for pair in range(HEADS // 2):
            ls = slice(pair * LANES, (pair + 1) * LANES)
            ka = jnp.concatenate([k_ref[j, ls, :], ck], axis=0)
            v_aug = jnp.concatenate([v_ref[j, ls, :], ones], axis=0)
            for h in (2 * pair, 2 * pair + 1):
                s = _dot(qa_ref[h], ka)
                if mask is not None:
                    s = jnp.where(mask, s, -jnp.inf)
                parts = [s[:, t * LANES:(t + 1) * LANES] for t in range(tq // LANES)]
                m_old = m_ref[h]
                m_new = jnp.maximum(m_old, jnp.max(functools.reduce(jnp.maximum, parts), axis=-1, keepdims=True))
                alpha = jnp.exp2(m_old - m_new)
                p = jnp.concatenate([jnp.exp2(t - m_new) for t in parts], axis=1).astype(BF16)
                m_ref[h] = m_new
                acc_ref[h] = jnp.concatenate([alpha, alpha], axis=1) * acc_ref[h] + _dot_nt(p, v_aug)

    def body(j, carry):
        block(j, None)
        return carry

    lax.fori_loop(0, i, body, 0)
    row = lax.broadcasted_iota(jnp.int32, (tq, tq), 0)
    col = lax.broadcasted_iota(jnp.int32, (tq, tq), 1)
    block(i, col <= row)
    for pair in range(HEADS // 2):
        lo = acc_ref[2 * pair, :, 0:LANES] / acc_ref[2 * pair, :, LANES:2 * LANES]
        hi = acc_ref[2 * pair + 1, :, 0:LANES] / acc_ref[2 * pair + 1, :, LANES:2 * LANES]
        o_ref[0, :, pair * LANES:(pair + 1) * LANES] = jnp.where(
            lax.shift_right_logical(lane, 6) == 0, lo, hi).astype(BF16)


def _fox_prompt(q, cq, ktb, vtb, ckt):
    b, s, _ = q.shape
    tq = KV_BLOCK
    assert s % tq == 0
    nkb = s // tq
    blk = lambda w: pl.BlockSpec((1, tq, w), lambda bi, i: (bi, i, 0))
    full = lambda r: pl.BlockSpec((nkb, r, tq), lambda bi, i: (bi, 0, 0))
    return pl.pallas_call(
        functools.partial(_fox_prompt_kernel, tq=tq),
        out_shape=jax.ShapeDtypeStruct((b, s, WIDTH), BF16),
        grid=(b, s // tq),
        in_specs=[blk(WIDTH), blk(LANES), full(WIDTH), full(WIDTH), full(LANES)],
        out_specs=blk(WIDTH),
        scratch_shapes=[pltpu.VMEM((HEADS, tq, 2 * LANES), BF16),
                        pltpu.VMEM((HEADS, tq, LANES), F32),
                        pltpu.VMEM((HEADS, tq, 2 * LANES), F32)],
        compiler_params=_params(("arbitrary", "arbitrary")),
        name="fox_prompt",
    )(q, cq, ktb, vtb, ckt)


def _fox_sample_kernel(pt_ref, q_ref, kn_ref, vn_ref, ckn_ref, *refs, t_new, pps, page):
    k_refs = refs[0:pps]
    v_refs = refs[pps:2 * pps]
    lf_refs = refs[2 * pps:3 * pps]
    o_ref = refs[3 * pps]
    qrow_ref, m_ref, l_ref, acc_ref, run_ref, pad_ref = refs[3 * pps + 1:]
    del pt_ref
    step = pl.program_id(1)
    n_rows = t_new * HEADS
    rowh = lax.broadcasted_iota(jnp.int32, (n_rows, WIDTH), 0) & (HEADS - 1)
    laneh = lax.shift_right_logical(lax.broadcasted_iota(jnp.int32, (n_rows, WIDTH), 1), 6)
    head_mask = rowh == laneh

    @pl.when(step == 0)
    def _():
        q = q_ref[0]
        rows = jnp.concatenate([jnp.broadcast_to(q[t:t + 1, :], (HEADS, WIDTH)) for t in range(t_new)],
                               axis=0)
        qrow_ref[...] = jnp.where(head_mask, rows, 0.0).astype(BF16)
        m_ref[...] = jnp.full(m_ref.shape, -jnp.inf, F32)
        l_ref[...] = jnp.zeros(l_ref.shape, F32)
        acc_ref[...] = jnp.zeros(acc_ref.shape, F32)
        run_ref[...] = jnp.zeros(run_ref.shape, F32)

    qrows = qrow_ref[...]
    lf = jnp.concatenate([r[0] for r in lf_refs], axis=0)
    newer = lax.broadcasted_iota(jnp.int32, (page, page), 0) > lax.broadcasted_iota(jnp.int32, (page, page), 1)
    suf = _dot3_rhs(lf, jnp.where(newer, 1.0, 0.0).astype(BF16))
    run = run_ref[...]
    scores = []
    for n in range(pps):
        blk = slice(n * HEADS, (n + 1) * HEADS)
        bias = suf[blk] + run
        run = run + (suf[blk][:, 0:1] + lf[blk][:, 0:1])
        scores.append(_dot(qrows, k_refs[n][0].astype(BF16)) + jnp.concatenate([bias] * t_new, axis=0))
    run_ref[...] = run
    m_old = m_ref[...]
    m_new = m_old
    for s in scores:
        m_new = jnp.maximum(m_new, jnp.max(s, axis=-1, keepdims=True))
    alpha = jnp.exp(m_old - m_new)
    l_new = alpha * l_ref[...]
    pv = None
    for n in range(pps):
        p = jnp.exp(scores[n] - m_new)
        l_new = l_new + jnp.sum(p, axis=-1, keepdims=True)
        part = _dot_nt(p.astype(BF16), v_refs[n][0].astype(BF16))
        pv = part if pv is None else pv + part
    m_ref[...] = m_new
    l_ref[...] = l_new
    acc_ref[...] = alpha * acc_ref[...] + pv

    @pl.when(step == pl.num_programs(1) - 1)
    def _():
        pad_ref[...] = jnp.zeros(pad_ref.shape, F32)
        pad_ref[0, 0:t_new, :] = kn_ref[0]
        pad_ref[1, 0:t_new, :] = vn_ref[0]
        pad_ref[2, 0:t_new, 0:LANES] = ckn_ref[0]
        kn = pad_ref[0].astype(BF16)
        vn = pad_ref[1].astype(BF16)
        ckn = pad_ref[2, :, 0:LANES].astype(BF16)
        r8 = lax.broadcasted_iota(jnp.int32, (n_rows, LANES), 0) & (HEADS - 1)
        l8 = lax.broadcasted_iota(jnp.int32, (n_rows, LANES), 1)
        sel = (lax.shift_right_logical(l8, 3) == r8) & ((l8 & (BIAS_STRIDE - 1)) >= BIAS_PARTS) \
            & ((l8 & (BIAS_STRIDE - 1)) < 2 * BIAS_PARTS)
        cqr = jnp.where(sel, 1.0, 0.0).astype(BF16)
        s = _dot_nt(jnp.concatenate([qrows, cqr], axis=1), jnp.concatenate([kn, ckn], axis=1))
        key = lax.broadcasted_iota(jnp.int32, (n_rows, page), 1)
        tok = lax.shift_right_logical(lax.broadcasted_iota(jnp.int32, (n_rows, page), 0), 3)
        s = jnp.where(key <= tok, s, -jnp.inf)
        _, l, acc = _softmax_step((m_ref[...], l_ref[...], acc_ref[...]), s, vn)
        o = jnp.where(head_mask, acc / l, 0.0)
        o_ref[0] = jnp.sum(o.reshape(t_new, HEADS, WIDTH), axis=1).astype(BF16)


def _fox_sample(q, kb, vb, ck, cache_k, cache_v, cache_lf_t, page_table):
    db, t_new, _ = q.shape
    n_pages = page_table.shape[1]
    n_pool, _, page = cache_k.shape
    pps = 16
    assert n_pages % pps == 0 and page == LANES and t_new <= HEADS
    n_rows = t_new * HEADS
    pt = page_table.reshape(-1)

    def page_idx(n):
        return lambda b, s, pt_ref: (pt_ref[b * n_pages + n_pages - 1 - (s * pps + n)], 0, 0)

    new = lambda w: pl.BlockSpec((1, t_new, w), lambda b, s, pt_ref: (b, 0, 0))
    in_specs = ([new(WIDTH), new(WIDTH), new(WIDTH), new(LANES)]
                + [pl.BlockSpec((1, WIDTH, page), page_idx(n)) for n in range(pps)]
                + [pl.BlockSpec((1, WIDTH, page), page_idx(n)) for n in range(pps)]
                + [pl.BlockSpec((1, HEADS, page), page_idx(n)) for n in range(pps)])
    grid_spec = pltpu.PrefetchScalarGridSpec(
        num_scalar_prefetch=1,
        grid=(db, n_pages // pps),
        in_specs=in_specs,
        out_specs=new(WIDTH),
        scratch_shapes=[pltpu.VMEM((n_rows, WIDTH), BF16),
                        pltpu.VMEM((n_rows, 1), F32), pltpu.VMEM((n_rows, 1), F32),
                        pltpu.VMEM((n_rows, WIDTH), F32), pltpu.VMEM((HEADS, 1), F32),
                        pltpu.VMEM((3, page, WIDTH), F32)],
    )
    return pl.pallas_call(
        functools.partial(_fox_sample_kernel, t_new=t_new, pps=pps, page=page),
        out_shape=jax.ShapeDtypeStruct((db, t_new, WIDTH), BF16),
        grid_spec=grid_spec,
        compiler_params=_params(("arbitrary", "arbitrary")),
        name="fox_sample",
    )(pt, q, kb, vb, ck, *([cache_k] * pps), *([cache_v] * pps), *([cache_lf_t] * pps))


def _mlp_kernel(x_ref, or_ref, of_ref, gt_ref, lg_ref, lb_ref, wb_ref, wo_ref, g1_ref, b1_ref,
                wu_ref, wd_ref, g2_ref, b2_ref, y_ref, *, alpha, ff_chunk):
    d = x_ref.shape[1]
    hp = _layer_norm(x_ref[...], lg_ref[...], lb_ref[...])
    gt = gt_ref[...]
    merged = (gt[:, :d].astype(F32) * _dot(or_ref[...], wb_ref[0:WIDTH, :])
              + gt[:, d:].astype(F32) * _dot(of_ref[...], wb_ref[WIDTH:2 * WIDTH, :]))
    x1 = _layer_norm(alpha * hp + _dot(merged.astype(BF16), wo_ref[...]), g1_ref[...], b1_ref[...])
    x1b = x1.astype(BF16)
    h = jnp.zeros_like(x1)
    for c in range(wu_ref.shape[1] // ff_chunk):
        cs = slice(c * ff_chunk, (c + 1) * ff_chunk)
        up = jnp.maximum(_dot(x1b, wu_ref[:, cs]), 0.0)
        h = h + _dot((up * up).astype(BF16), wd_ref[cs, :])
    y_ref[...] = _layer_norm(alpha * x1 + h, g2_ref[...], b2_ref[...])


def _mlp(x2d, o_r, o_f, gates, ln_g, ln_b, wb, wo, g1, b1, wu, wd, g2, b2, alpha):
    m, d = x2d.shape
    tm = min(512, m)
    assert m % tm == 0
    row = lambda w: pl.BlockSpec((tm, w), lambda i: (i, 0))
    consts = (ln_g, ln_b, wb, wo, g1, b1, wu, wd, g2, b2)
    return pl.pallas_call(
        functools.partial(_mlp_kernel, alpha=alpha, ff_chunk=1024),
        out_shape=jax.ShapeDtypeStruct((m, d), F32),
        grid=(m // tm,),
        in_specs=[row(d), row(WIDTH), row(WIDTH), row(2 * d)] + [_const_spec(c.shape) for c in consts],
        out_specs=row(d),
        compiler_params=_params(("arbitrary",)),
        name="mlp",
    )(x2d, o_r, o_f, gates, *consts)


def kernel(x_prompt, x_sample, state_wkv, state_shift, cache_k, cache_v, cache_logf, page_table, ln_in_g, ln_in_b, w_in, rwkv_mu, rwkv_w0, rwkv_w2, rwkv_a0, rwkv_a2, rwkv_g2, rwkv_k_k, rwkv_k_a, rwkv_r_k, rwkv_gn_g, rwkv_gn_b, fox_b_f, w_branch, w_out, ln1_g, ln1_b, w_up, w_down, ln2_g, ln2_b):
    depth = w_in.shape[0]
    assert depth == 1, "the entry LayerNorm is fused into the first layer's kernels"
    bp, seq, d = x_prompt.shape
    db, dseq, _ = x_sample.shape
    n_pool, page = cache_k.shape[1], cache_k.shape[2]
    alpha = (2.0 * depth) ** 0.25
    row2 = lambda t: t.reshape(1, -1).astype(F32)
    ln_g, ln_b = row2(ln_in_g), row2(ln_in_b)

    l = 0
    w = w_in[l]
    fox_off = RWKV_PROJ
    wr = w[:, :RWKV_PROJ].astype(BF16)
    wq = w[:, fox_off:fox_off + WIDTH].astype(BF16)
    wt = w.T
    wkt = wt[fox_off + WIDTH:fox_off + 2 * WIDTH].astype(BF16)
    wvt = wt[fox_off + 2 * WIDTH:fox_off + 3 * WIDTH].astype(BF16)
    wft = jnp.pad(wt[fox_off + 3 * WIDTH:fox_off + 3 * WIDTH + HEADS], ((0, LANES - HEADS), (0, 0))).astype(BF16)
    wg = w[:, fox_off + 3 * WIDTH + HEADS:].astype(BF16)
    bf_col = jnp.pad(fox_b_f[l].reshape(HEADS, 1), ((0, LANES - HEADS), (0, 0))).astype(F32)
    proj_consts = (ln_g, ln_b, wq, wkt, wvt, wft, bf_col, wg)
    zl = jnp.zeros((DECAY_LORA, WIDTH), BF16)
    w2p = jnp.concatenate([rwkv_w2[l].astype(BF16), zl], axis=0)
    a2p = jnp.concatenate([zl, rwkv_a2[l].astype(BF16)], axis=0)
    rwkv_consts = (ln_g, ln_b, wr, row2(rwkv_mu[l]), row2(rwkv_w0[l]), w2p, row2(rwkv_a0[l]), a2p,
                   rwkv_g2[l].astype(BF16), row2(rwkv_k_k[l]), row2(rwkv_k_a[l]), row2(rwkv_r_k[l]),
                   row2(rwkv_gn_g[l]), row2(rwkv_gn_b[l]))
    mlp_consts = (ln_g, ln_b, w_branch[l].astype(BF16), w_out[l].astype(BF16), row2(ln1_g[l]),
                  row2(ln1_b[l]), w_up[l].astype(BF16), w_down[l].astype(BF16), row2(ln2_g[l]),
                  row2(ln2_b[l]))

    xp2 = x_prompt.reshape(bp * seq, d)
    q, kt32, vt32, ktb, vtb, lft, cq, ckt, gates = _proj(xp2, *proj_consts, seq, LOG2_E)
    to_seq = lambda t: t.reshape(bp, seq, t.shape[-1])
    o_f = _fox_prompt(to_seq(q), to_seq(cq), ktb, vtb, ckt)
    o_r, p_shift, p_wkv = _rwkv(x_prompt, seq, *rwkv_consts,
                                jnp.zeros((bp, RWKV_PROJ), F32),
                                jnp.zeros((bp, HEADS, HEAD_DIM, HEAD_DIM), F32))
    y_prompt = _mlp(xp2, o_r.reshape(bp * seq, WIDTH), o_f.reshape(bp * seq, WIDTH), gates,
                    *mlp_consts, alpha).reshape(bp, seq, d)
    heads_last = lambda t: jnp.transpose(t.reshape(bp, HEADS, HEAD_DIM, seq), (0, 3, 1, 2))[None]
    p_k = heads_last(kt32)
    p_v = heads_last(vt32)
    p_lf = jnp.swapaxes(lft, 1, 2)[None]

    xs2 = x_sample.reshape(db * dseq, d)
    q, kt32, vt32, _, _, lft, cq, ckt, gates = _proj(xs2, *proj_consts, dseq, 1.0)
    rows = lambda t: jnp.swapaxes(t, 1, 2).reshape(db * dseq, t.shape[1])
    k32, v32, lf, ck = rows(kt32), rows(vt32), rows(lft), rows(ckt)
    to_seq = lambda t: t.reshape(db, dseq, t.shape[-1])
    o_f = _fox_sample(to_seq(q.astype(F32)), to_seq(k32), to_seq(v32), to_seq(ck.astype(F32)),
                      jnp.transpose(cache_k[l], (0, 2, 3, 1)).reshape(n_pool, WIDTH, page),
                      jnp.transpose(cache_v[l], (0, 2, 3, 1)).reshape(n_pool, WIDTH, page),
                      jnp.swapaxes(cache_logf[l], 1, 2), page_table)
    xs_pad = jnp.pad(x_sample, ((0, 0), (0, CHUNK - dseq), (0, 0)))
    o_r, s_shift, s_wkv = _rwkv(xs_pad, dseq, *rwkv_consts, state_shift[l], state_wkv[l])
    y_sample = _mlp(xs2, o_r[:, :dseq].reshape(db * dseq, WIDTH), o_f.reshape(db * dseq, WIDTH), gates,
                    *mlp_consts, alpha).reshape(db, dseq, d)
    s_k = k32.reshape(1, db, dseq, HEADS, HEAD_DIM)
    s_v = v32.reshape(1, db, dseq, HEADS, HEAD_DIM)
    s_lf = lf.reshape(1, db, dseq, HEADS)

    return (y_prompt, y_sample, p_wkv[None], p_shift[None], p_k, p_v, p_lf,
            s_wkv[None], s_shift[None], s_k, s_v, s_lf)
```

```python
import functools
import math

import numpy as np
import jax
import jax.numpy as jnp
from jax import lax
from jax.experimental import pallas as pl
from jax.experimental.pallas import tpu as pltpu

F32 = jnp.float32
BF16 = jnp.bfloat16

HEADS = 8
HEAD_DIM = 64
WIDTH = HEADS * HEAD_DIM
DECAY_LORA = 64
AAA_LORA = 64
GATE_LORA = 128
RWKV_PROJ = 3 * WIDTH + DECAY_LORA + AAA_LORA + GATE_LORA
LORA_OFF = 3 * WIDTH
GATE_LORA_OFF = LORA_OFF + DECAY_LORA + AAA_LORA
GN_EPS = 64e-5
LN_EPS = 1e-5
QK_SCALE = HEAD_DIM ** -0.5
LOG2_E = math.log2(math.e)

LANES = 128
KV_BLOCK = 256
QUERY_BLOCK = 512
CHUNK = 64
GROUP_HEADS = 4
GROUP_W = GROUP_HEADS * HEAD_DIM
N_GROUPS = HEADS // GROUP_HEADS
BIAS_PARTS = 3
BIAS_STRIDE = 8
VMEM_LIMIT = 56 * 1024 * 1024


def _dot(a, b):
    return jnp.dot(a, b, preferred_element_type=F32)


def _dot_nt(a, b):
    return lax.dot_general(a, b, (((1,), (1,)), ((), ())), preferred_element_type=F32)


def _dot_tn(a, b):
    return lax.dot_general(a, b, (((0,), (0,)), ((), ())), preferred_element_type=F32)


def _layer_norm(x, g, b):
    mu = jnp.mean(x, -1, keepdims=True)
    xc = x - mu
    var = jnp.mean(xc * xc, -1, keepdims=True)
    return xc * lax.rsqrt(var + LN_EPS) * g + b


def _split3(x):
    p1 = x.astype(BF16)
    r1 = x - p1.astype(F32)
    p2 = r1.astype(BF16)
    p3 = (r1 - p2.astype(F32)).astype(BF16)
    return p1, p2, p3


def _dot3(m_bf16, x):
    p1, p2, p3 = _split3(x)
    return _dot(m_bf16, p1) + _dot(m_bf16, p2) + _dot(m_bf16, p3)


def _dot3_rhs(x, m_bf16):
    p1, p2, p3 = _split3(x)
    return _dot(p1, m_bf16) + _dot(p2, m_bf16) + _dot(p3, m_bf16)


def _seq_tri(n, seq_len, upper=False):
    row = lax.broadcasted_iota(jnp.int32, (n, n), 0)
    col = lax.broadcasted_iota(jnp.int32, (n, n), 1)
    keep = (row <= col) if upper else (col <= row)
    if seq_len < n:
        sh = int(math.log2(seq_len))
        assert 1 << sh == seq_len
        keep = keep & (lax.shift_right_logical(row, sh) == lax.shift_right_logical(col, sh))
    return jnp.where(keep, 1.0, 0.0).astype(BF16)


def _const_spec(shape):
    nd = len(shape)
    return pl.BlockSpec(shape, lambda *_: (0,) * nd, pipeline_mode=pl.Buffered(1))


def _params(sem):
    return pltpu.CompilerParams(dimension_semantics=sem, vmem_limit_bytes=VMEM_LIMIT)


def _bias_consts():
    e = np.zeros((2 * BIAS_PARTS, LANES, LANES), np.float32)
    ones_q = np.zeros((1, LANES), np.float32)
    ones_k = np.zeros((1, LANES), np.float32)
    for h in range(HEADS):
        for j in range(BIAS_PARTS):
            e[j, h, BIAS_STRIDE * h + j] = 1.0
            e[BIAS_PARTS + j, h, BIAS_STRIDE * h + BIAS_PARTS + j] = 1.0
            ones_q[0, BIAS_STRIDE * h + BIAS_PARTS + j] = 1.0
            ones_k[0, BIAS_STRIDE * h + j] = 1.0
    return jnp.asarray(e, BF16), jnp.asarray(ones_q), jnp.asarray(ones_k)


def _proj_kernel(x_ref, g_ref, b_ref, wq_ref, wkt_ref, wvt_ref, wft_ref, bf_ref, wg_ref,
                 e_ref, et_ref, oq_ref, ok_ref,
                 q_ref, kt_ref, vt_ref, ktb_ref, vtb_ref, lft_ref, cq_ref, ckt_ref, gt_ref,
                 carry_ref, *, seq_len, tm, logit_scale):
    xn = _layer_norm(x_ref[...], g_ref[...], b_ref[...]).astype(BF16)
    q_ref[...] = (_dot(xn, wq_ref[...]) * (QK_SCALE * logit_scale)).astype(BF16)
    gt_ref[...] = jax.nn.sigmoid(_dot(xn, wg_ref[...])).astype(BF16)
    for w_ref, o32_ref, ob_ref in ((wkt_ref, kt_ref, ktb_ref), (wvt_ref, vt_ref, vtb_ref)):
        t = _dot_nt(w_ref[...], xn)
        o32_ref[0] = t
        for jb in range(tm // KV_BLOCK):
            ob_ref[jb] = t[:, jb * KV_BLOCK:(jb + 1) * KV_BLOCK].astype(BF16)

    rowi = lax.broadcasted_iota(jnp.int32, (LANES, tm), 0)
    lft = jax.nn.log_sigmoid(_dot_nt(wft_ref[...], xn) + bf_ref[...])
    lft = jnp.where(rowi < HEADS, lft, 0.0)
    lft_ref[0] = lft[:HEADS, :]

    ct = _dot3_rhs(lft, _seq_tri(tm, seq_len, upper=True))
    if seq_len > tm:
        @pl.when(pl.program_id(0) % (seq_len // tm) == 0)
        def _():
            carry_ref[...] = jnp.zeros_like(carry_ref)
        ct = ct + carry_ref[...]
        carry_ref[...] = ct[:, tm - 1:tm]

    if logit_scale != 1.0:
        ct = ct * logit_scale
    c1t, c2t, c3t = _split3(ct)
    ckt = ok_ref[...] - (_dot(et_ref[0], c1t) + _dot(et_ref[1], c2t) + _dot(et_ref[2], c3t))
    for jb in range(tm // KV_BLOCK):
        ckt_ref[jb] = ckt[:, jb * KV_BLOCK:(jb + 1) * KV_BLOCK].astype(BF16)
    c1, c2, c3 = _split3(ct.T)
    cq = _dot(c1, e_ref[0]) + _dot(c2, e_ref[1]) + _dot(c3, e_ref[2]) + oq_ref[...]
    cq_ref[...] = cq.astype(BF16)


def _proj(x2d, ln_g, ln_b, wq, wkt, wvt, wft, bf_col, wg, seq_len, logit_scale):
    m, d = x2d.shape
    tm = min(512, m)
    assert m % tm == 0 and (seq_len % tm == 0 or tm % seq_len == 0) and tm % KV_BLOCK == 0
    e, ones_q, ones_k = _bias_consts()
    e_q = e[:BIAS_PARTS]
    e_kt = jnp.swapaxes(e[BIAS_PARTS:], 1, 2)
    ones_k_col = ones_k.reshape(LANES, 1)
    if seq_len >= tm:
        tps = seq_len // tm
        groups, cols = m // seq_len, seq_len
        tr = lambda r: pl.BlockSpec((1, r, tm), lambda i: (i // tps, 0, i % tps))
    else:
        groups, cols = m // tm, tm
        tr = lambda r: pl.BlockSpec((1, r, tm), lambda i: (i, 0, 0))
    nkb = tm // KV_BLOCK
    trb = lambda r: pl.BlockSpec((nkb, r, KV_BLOCK), lambda i: (i, 0, 0))
    row = lambda w: pl.BlockSpec((tm, w), lambda i: (i, 0))
    out_shape = (
        jax.ShapeDtypeStruct((m, WIDTH), BF16),
        jax.ShapeDtypeStruct((groups, WIDTH, cols), F32),
        jax.ShapeDtypeStruct((groups, WIDTH, cols), F32),
        jax.ShapeDtypeStruct((m // KV_BLOCK, WIDTH, KV_BLOCK), BF16),
        jax.ShapeDtypeStruct((m // KV_BLOCK, WIDTH, KV_BLOCK), BF16),
        jax.ShapeDtypeStruct((groups, HEADS, cols), F32),
        jax.ShapeDtypeStruct((m, LANES), BF16),
        jax.ShapeDtypeStruct((m // KV_BLOCK, LANES, KV_BLOCK), BF16),
        jax.ShapeDtypeStruct((m, 2 * d), BF16),
    )
    consts = (ln_g, ln_b, wq, wkt, wvt, wft, bf_col, wg, e_q, e_kt, ones_q, ones_k_col)
    return pl.pallas_call(
        functools.partial(_proj_kernel, seq_len=seq_len, tm=tm, logit_scale=logit_scale),
        out_shape=out_shape,
        grid=(m // tm,),
        in_specs=[row(d)] + [_const_spec(c.shape) for c in consts],
        out_specs=(row(WIDTH), tr(WIDTH), tr(WIDTH), trb(WIDTH), trb(WIDTH), tr(HEADS),
                   row(LANES), trb(LANES), row(2 * d)),
        scratch_shapes=[pltpu.VMEM((LANES, 1), F32)],
        compiler_params=_params(("arbitrary",)),
        name="proj",
    )(x2d, *consts)


def _group_mask(rows):
    r = lax.broadcasted_iota(jnp.int32, (rows, GROUP_W), 0)
    c = lax.broadcasted_iota(jnp.int32, (rows, GROUP_W), 1)
    sh = int(math.log2(HEAD_DIM))
    return (lax.shift_right_logical(r, sh) & (GROUP_HEADS - 1)) == lax.shift_right_logical(c, sh)


def _bd(y, mask):
    yb = y.astype(BF16)
    return jnp.where(mask, jnp.concatenate([yb] * GROUP_HEADS, axis=0), jnp.zeros((), BF16))


def _state_to_bd(st, mask_bd):
    out = []
    for g in range(N_GROUPS):
        blk = st[g * GROUP_W:(g + 1) * GROUP_W, :]
        out.append(jnp.where(mask_bd, jnp.concatenate([blk] * GROUP_HEADS, axis=1), 0.0))
    return out


def _state_from_bd(state, mask_bd):
    rows = []
    for sg in state:
        sm = jnp.where(mask_bd, sg, 0.0)
        acc = sm[:, 0:HEAD_DIM]
        for h in range(1, GROUP_HEADS):
            acc = acc + sm[:, h * HEAD_DIM:(h + 1) * HEAD_DIM]
        rows.append(acc)
    return jnp.concatenate(rows, axis=0)


def _rwkv_kernel(x_ref, g_ref, b_ref, wr_ref, mu_ref, w0_ref, w2_ref, a0_ref, a2_ref, g2_ref,
                 kk_ref, ka_ref, rk_ref, gng_ref, gnb_ref, ones_ref, shift_ref, state_ref,
                 o_ref, shift_out_ref, state_out_ref,
                 s_ref, prev_ref, *, tm, t_real, chained):
    nc = tm // CHUNK
    mask_bd = _group_mask(GROUP_W)
    rowi = lax.broadcasted_iota(jnp.int32, (tm, 1), 0)

    if chained:
        @pl.when(pl.program_id(1) == 0)
        def _():
            prev_ref[...] = shift_ref[0]
            for g, sg in enumerate(_state_to_bd(state_ref[0], mask_bd)):
                s_ref[g] = sg

    xn = _layer_norm(x_ref[...].reshape(tm, x_ref.shape[-1]), g_ref[...], b_ref[...]).astype(BF16)
    p = _dot(xn, wr_ref[...])
    if chained:
        p_prev = jnp.where(rowi == 0, prev_ref[...], pltpu.roll(p, 1, 0))
        prev_ref[...] = p[tm - 1:tm, :]
        shift_out_ref[0] = p[tm - 1:tm, :]
    else:
        first = jnp.concatenate([jnp.broadcast_to(shift_ref[c], (CHUNK, RWKV_PROJ)) for c in range(nc)], axis=0)
        p_prev = jnp.where((rowi & (CHUNK - 1)) == 0, first, pltpu.roll(p, 1, 0))
        for c in range(nc):
            shift_out_ref[c] = p[c * CHUNK + t_real - 1:c * CHUNK + t_real, :]
    pm = p + (p_prev - p) * mu_ref[...]

    r = pm[:, 0:WIDTH]
    k = pm[:, WIDTH:2 * WIDTH]
    v = pm[:, 2 * WIDTH:3 * WIDTH]
    dwa = pm[:, LORA_OFF:GATE_LORA_OFF]
    dg = pm[:, GATE_LORA_OFF:RWKV_PROJ]
    w_log = -jax.nn.softplus(-(w0_ref[...] + _dot(jnp.tanh(dwa).astype(BF16), w2_ref[...]))) - 0.5
    lw = -jnp.exp(w_log)
    a = jax.nn.sigmoid(a0_ref[...] + _dot(dwa.astype(BF16), a2_ref[...]))
    gate = _dot(jax.nn.sigmoid(dg).astype(BF16), g2_ref[...])
    kk = k * kk_ref[...]
    ss = _dot((kk * kk).astype(BF16), ones_ref[...])
    kk = kk * lax.rsqrt(jnp.maximum(ss, 1e-24))
    k = k * (1.0 + (a - 1.0) * ka_ref[...])
    bonus = _dot((r * k * rk_ref[...]).astype(BF16), ones_ref[...]) * v
    kka = kk * a
    if not chained and t_real < CHUNK:
        live = (rowi & (CHUNK - 1)) < t_real
        lw = jnp.where(live, lw, 0.0)
        r, k, v, kk, kka = (jnp.where(live, t, 0.0) for t in (r, k, v, kk, kka))

    tri3 = jnp.concatenate([_seq_tri(CHUNK, CHUNK)] * BIAS_PARTS, axis=1)
    lw_parts = _split3(lw)
    cl = jnp.concatenate(
        [_dot(tri3, jnp.concatenate([t[c * CHUNK:(c + 1) * CHUNK, :] for t in lw_parts], axis=0))
         for c in range(nc)], axis=0) if nc > 1 else _dot(tri3, jnp.concatenate(lw_parts, axis=0))
    e_in = jnp.exp(cl)
    e_ex = jnp.exp(cl - lw)
    e_ng = jnp.exp(-cl)
    rt = r * e_in
    at = -kk * e_ex
    bt = kka * e_ng
    kt = k * e_ng

    row = lax.broadcasted_iota(jnp.int32, (CHUNK, GROUP_W), 0)
    col = lax.broadcasted_iota(jnp.int32, (CHUNK, GROUP_W), 1) & (CHUNK - 1)
    strict = col < row
    incl = col <= row
    blk16 = lax.shift_right_logical(row, 4) == lax.shift_right_logical(col, 4)
    blk32 = lax.shift_right_logical(row, 5) == lax.shift_right_logical(col, 5)

    def mm(x, y):
        return _dot(x.astype(BF16), _bd(y, mask_bd))

    probs = [(c, g) for c in range(nc) for g in range(N_GROUPS)]
    sl = lambda x, c, g: x[c * CHUNK:(c + 1) * CHUNK, g * GROUP_W:(g + 1) * GROUP_W]
    each = lambda f, *ls: [f(*xs) for xs in zip(*ls)]
    rt_, at_, bt_, kt_, v_ = ([sl(x, c, g) for c, g in probs] for x in (rt, at, bt, kt, v))
    cl_end = [cl[(c + 1) * CHUNK - 1:(c + 1) * CHUNK, :] for c in range(nc)]
    g_end = [jnp.exp(ce) for ce in cl_end]
    e_end = [jnp.exp(cl_end[c] - cl[c * CHUNK:(c + 1) * CHUNK, :]) for c in range(nc)]
    bh_ = [sl(kka, c, g) * e_end[c][:, g * GROUP_W:(g + 1) * GROUP_W] for c, g in probs]
    kh_ = [sl(k, c, g) * e_end[c][:, g * GROUP_W:(g + 1) * GROUP_W] for c, g in probs]

    ar_ = each(lambda a_, r_: jnp.concatenate([a_, r_], axis=0).astype(BF16), at_, rt_)
    gb_ = each(lambda ar, b_: _dot_nt(ar, _bd(b_, mask_bd)), ar_, bt_)
    gk_ = each(lambda ar, k_: _dot_nt(ar, _bd(k_, mask_bd)), ar_, kt_)
    n_ab = [jnp.where(strict, x[:CHUNK], 0.0) for x in gb_]
    a_ak = [jnp.where(strict, x[:CHUNK], 0.0) for x in gk_]
    p_rb = [jnp.where(incl, x[CHUNK:], 0.0) for x in gb_]
    p_rk = [jnp.where(incl, x[CHUNK:], 0.0) for x in gk_]
    stack = lambda a_, b_: jnp.concatenate([a_, b_], axis=0)
    top = lambda xs: [x[:CHUNK] for x in xs]
    bot = lambda xs: [x[CHUNK:] for x in xs]
    uvp = each(mm, each(stack, a_ak, p_rk), v_)
    uv_, prkv = top(uvp), bot(uvp)

    n_d = [jnp.where(blk16, x, 0.0) for x in n_ab]
    n_1 = [jnp.where(blk32 & jnp.logical_not(blk16), x, 0.0) for x in n_ab]
    n_2 = [jnp.where(blk32, 0.0, x) for x in n_ab]
    add = lambda *xs: functools.reduce(lambda a_, b_: a_ + b_, xs)
    n2 = each(mm, n_d, n_d)
    n34 = each(mm, each(stack, n_d, n2), n2)
    n3, n4 = top(n34), bot(n34)
    q4 = each(add, n_d, n2, n3)
    m48 = each(mm, each(stack, q4, n4), n4)
    q8 = each(add, q4, n4, top(m48))
    n8 = bot(m48)
    t16 = each(add, q8, n8, each(mm, q8, n8))
    x1 = each(add, n_1, each(mm, n_1, t16))
    t32 = each(add, t16, x1, each(mm, t16, x1))
    x2 = each(add, n_2, each(mm, n_2, t32))
    t_ = each(add, t32, x2, each(mm, t32, x2))

    wt_ = each(add, at_, each(mm, t_, at_))
    ut_ = each(add, uv_, each(mm, t_, uv_))
    mw_ = each(lambda w_, b_: jnp.where(mask_bd, _dot_tn(w_.astype(BF16), b_.astype(BF16)), 0.0).astype(BF16),
               wt_, bh_)
    dd_ = each(lambda u_, vv, b_, k_: jnp.where(mask_bd, _dot_tn(
        jnp.concatenate([u_, vv], axis=0).astype(BF16), jnp.concatenate([b_, k_], axis=0).astype(BF16)), 0.0),
               ut_, v_, bh_, kh_)
    qq_ = each(lambda r_, x: (r_ + x).astype(BF16), rt_, each(mm, p_rb, wt_))
    o0_ = each(add, each(mm, p_rb, ut_), prkv)

    if chained:
        state = [s_ref[g] for g in range(N_GROUPS)]
    o_chunks = []
    for c in range(nc):
        if not chained:
            state = _state_to_bd(state_ref[c], mask_bd)
        o_groups = []
        for g in range(N_GROUPS):
            j = c * N_GROUPS + g
            s0 = state[g]
            s0b = s0.astype(BF16)
            o_groups.append(_dot_nt(qq_[j], s0b) + o0_[j])
            state[g] = s0 * g_end[c][:, g * GROUP_W:(g + 1) * GROUP_W] + _dot(s0b, mw_[j]) + dd_[j]
        o_chunks.append(jnp.concatenate(o_groups, axis=1))
        if not chained:
            state_out_ref[c] = _state_from_bd(state, mask_bd)
    if chained:
        for g in range(N_GROUPS):
            s_ref[g] = state[g]
    o = jnp.concatenate(o_chunks, axis=0) if len(o_chunks) > 1 else o_chunks[0]

    inv_n = 1.0 / HEAD_DIM
    mean = _dot(o.astype(BF16), ones_ref[...]) * inv_n
    oc = o - mean
    var = _dot((oc * oc).astype(BF16), ones_ref[...]) * inv_n
    on = oc * lax.rsqrt(var + GN_EPS) * gng_ref[...] + gnb_ref[...]
    o_ref[...] = ((on + bonus) * gate).astype(BF16).reshape(o_ref.shape)

    if chained:
        @pl.when(pl.program_id(1) == pl.num_programs(1) - 1)
        def _():
            state_out_ref[0] = _state_from_bd([s_ref[g] for g in range(N_GROUPS)], mask_bd)


def _rwkv(x3d, t_real, ln_g, ln_b, wr, mu, w0, w2p, a0, a2p, g2, k_k, k_a, r_k, gn_g, gn_b,
          shift_in, state_in):
    nseq, t, d = x3d.shape
    chained = t_real == t
    if chained:
        tm, nb = min(512, t), 1
        assert t % tm == 0 and tm % CHUNK == 0
        grid = (nseq, t // tm)
        x_spec = pl.BlockSpec((1, tm, d), lambda b, i: (b, i, 0))
        o_spec = pl.BlockSpec((1, tm, WIDTH), lambda b, i: (b, i, 0))
    else:
        nb = min(8, nseq)
        tm = nb * CHUNK
        assert t == CHUNK and nseq % nb == 0
        grid = (nseq // nb, 1)
        x_spec = pl.BlockSpec((nb, CHUNK, d), lambda b, i: (b, 0, 0))
        o_spec = pl.BlockSpec((nb, CHUNK, WIDTH), lambda b, i: (b, 0, 0))
    ones_bd = jnp.asarray(np.kron(np.eye(HEADS, dtype=np.float32),
                                  np.ones((HEAD_DIM, HEAD_DIM), np.float32)), BF16)
    consts = (ln_g, ln_b, wr, mu, w0, w2p, a0, a2p, g2, k_k, k_a, r_k, gn_g, gn_b, ones_bd)
    state2d = state_in.reshape(nseq, HEADS * HEAD_DIM, HEAD_DIM)
    shift3d = shift_in.reshape(nseq, 1, RWKV_PROJ)
    seq_spec = lambda shape: pl.BlockSpec((nb,) + shape, lambda b, i: (b, 0, 0))
    o, shift_out, state_out = pl.pallas_call(
        functools.partial(_rwkv_kernel, tm=tm, t_real=t_real, chained=chained),
        out_shape=(jax.ShapeDtypeStruct((nseq, t, WIDTH), BF16),
                   jax.ShapeDtypeStruct((nseq, 1, RWKV_PROJ), F32),
                   jax.ShapeDtypeStruct((nseq, HEADS * HEAD_DIM, HEAD_DIM), F32)),
        grid=grid,
        in_specs=[x_spec] + [_const_spec(c.shape) for c in consts]
                 + [seq_spec((1, RWKV_PROJ)), seq_spec((HEADS * HEAD_DIM, HEAD_DIM))],
        out_specs=(o_spec, seq_spec((1, RWKV_PROJ)), seq_spec((HEADS * HEAD_DIM, HEAD_DIM))),
        scratch_shapes=[pltpu.VMEM((N_GROUPS, GROUP_W, GROUP_W), F32),
                        pltpu.VMEM((1, RWKV_PROJ), F32)],
        compiler_params=_params(("arbitrary", "arbitrary")),
        name="rwkv",
    )(x3d, *consts, shift3d, state2d)
    return (o, shift_out.reshape(nseq, RWKV_PROJ),
            state_out.reshape(nseq, HEADS, HEAD_DIM, HEAD_DIM))


def _softmax_step(carry, s, v_blk):
    m, l, acc = carry
    m_new = jnp.maximum(m, jnp.max(s, axis=-1, keepdims=True))
    alpha = jnp.exp(m - m_new)
    p = jnp.exp(s - m_new)
    l = alpha * l + jnp.sum(p, axis=-1, keepdims=True)
    acc = alpha * acc + _dot(p.astype(BF16), v_blk)
    return m_new, l, acc


def _fox_prompt_kernel(q_ref, cq_ref, k_ref, v_ref, ck_ref, o_ref, qa_ref, m_ref, acc_ref, *, tq):
    i = pl.program_id(1)
    lane = lax.broadcasted_iota(jnp.int32, (tq, LANES), 1)
    cq = cq_ref[0]
    zero = jnp.zeros((), BF16)
    for h in range(HEADS):
        q_pair = q_ref[0, :, (h // 2) * LANES:(h // 2 + 1) * LANES]
        q_h = jnp.where(lax.shift_right_logical(lane, 6) == (h % 2), q_pair, zero)
        c_h = jnp.where(lax.shift_right_logical(lane, 3) == h, cq, zero)
        qa_ref[h] = jnp.concatenate([q_h, c_h], axis=1)
    m_ref[...] = jnp.full(m_ref.shape, -jnp.inf, F32)
    acc_ref[...] = jnp.zeros(acc_ref.shape, F32)
    tk = KV_BLOCK
    ones = jnp.ones((LANES, tk), BF16)

    def block(j, r0, mask):
        ck = ck_ref[j]
        for pair in range(HEADS // 2):
            ls = slice(pair * LANES, (pair + 1) * LANES)
            ka = jnp.concatenate([k_ref[j, ls, :], ck], axis=0)
            v_aug = jnp.concatenate([v_ref[j, ls, :], ones], axis=0)
            for h in (2 * pair, 2 * pair + 1):
                s = _dot(qa_ref[h, r0:, :], ka)
                if mask is not None:
                    s = jnp.where(mask, s, -jnp.inf)
                parts = [s[:, t * LANES:(t + 1) * LANES] for t in range(tk // LANES)]
                m_old = m_ref[h, r0:, :]
                m_new = jnp.maximum(m_old, jnp.max(functools.reduce(jnp.maximum, parts), axis=-1, keepdims=True))
                alpha = jnp.exp2(m_old - m_new)
                p = jnp.concatenate([jnp.exp2(t - m_new) for t in parts], axis=1).astype(BF16)
                m_ref[h, r0:, :] = m_new
                acc_ref[h, r0:, :] = (jnp.concatenate([alpha, alpha], axis=1) * acc_ref[h, r0:, :]
                                      + _dot_nt(p, v_aug))

    def body(j, carry):
        block(j, 0, None)
        return carry

    n_full = i * (tq // tk)
    lax.fori_loop(0, n_full, body, 0)
    for d in range(tq // tk):
        rows = tq - d * tk
        row = lax.broadcasted_iota(jnp.int32, (rows, tk), 0)
        col = lax.broadcasted_iota(jnp.int32, (rows, tk), 1)
        block(n_full + d, d * tk, col <= row)
    for pair in range(HEADS // 2):
        lo = acc_ref[2 * pair, :, 0:LANES] / acc_ref[2 * pair, :, LANES:2 * LANES]
        hi = acc_ref[2 * pair + 1, :, 0:LANES] / acc_ref[2 * pair + 1, :, LANES:2 * LANES]
        o_ref[0, :, pair * LANES:(pair + 1) * LANES] = jnp.where(
            lax.shift_right_logical(lane, 6) == 0, lo, hi).astype(BF16)


def _fox_prompt(q, cq, ktb, vtb, ckt):
    b, s, _ = q.shape
    tq = min(QUERY_BLOCK, s)
    assert s % tq == 0 and tq % KV_BLOCK == 0
    nkb = s // KV_BLOCK
    blk = lambda w: pl.BlockSpec((1, tq, w), lambda bi, i: (bi, i, 0))
    full = lambda r: pl.BlockSpec((nkb, r, KV_BLOCK), lambda bi, i: (bi, 0, 0))
    return pl.pallas_call(
        functools.partial(_fox_prompt_kernel, tq=tq),
        out_shape=jax.ShapeDtypeStruct((b, s, WIDTH), BF16),
        grid=(b, s // tq),
        in_specs=[blk(WIDTH), blk(LANES), full(WIDTH), full(WIDTH), full(LANES)],
        out_specs=blk(WIDTH),
        scratch_shapes=[pltpu.VMEM((HEADS, tq, 2 * LANES), BF16),
                        pltpu.VMEM((HEADS, tq, LANES), F32),
                        pltpu.VMEM((HEADS, tq, 2 * LANES), F32)],
        compiler_params=_params(("arbitrary", "arbitrary")),
        name="fox_prompt",
    )(q, cq, ktb, vtb, ckt)


def _fox_sample_kernel(pt_ref, q_ref, kn_ref, vn_ref, ckn_ref, *refs, t_new, pps, page):
    k_refs = refs[0:pps]
    v_refs = refs[pps:2 * pps]
    lf_refs = refs[2 * pps:3 * pps]
    o_ref = refs[3 * pps]
    qrow_ref, m_ref, l_ref, acc_ref, run_ref, pad_ref = refs[3 * pps + 1:]
    del pt_ref
    step = pl.program_id(1)
    n_rows = t_new * HEADS
    rowh = lax.broadcasted_iota(jnp.int32, (n_rows, WIDTH), 0) & (HEADS - 1)
    laneh = lax.shift_right_logical(lax.broadcasted_iota(jnp.int32, (n_rows, WIDTH), 1), 6)
    head_mask = rowh == laneh

    @pl.when(step == 0)
    def _():
        q = q_ref[0]
        rows = jnp.concatenate([jnp.broadcast_to(q[t:t + 1, :], (HEADS, WIDTH)) for t in range(t_new)],
                               axis=0)
        qrow_ref[...] = jnp.where(head_mask, rows, 0.0).astype(BF16)
        m_ref[...] = jnp.full(m_ref.shape, -jnp.inf, F32)
        l_ref[...] = jnp.zeros(l_ref.shape, F32)
        acc_ref[...] = jnp.zeros(acc_ref.shape, F32)
        run_ref[...] = jnp.zeros(run_ref.shape, F32)

    qrows = qrow_ref[...]
    lf = jnp.concatenate([r[0] for r in lf_refs], axis=0)
    newer = lax.broadcasted_iota(jnp.int32, (page, page), 0) > lax.broadcasted_iota(jnp.int32, (page, page), 1)
    suf = _dot3_rhs(lf, jnp.where(newer, 1.0, 0.0).astype(BF16))
    run = run_ref[...]
    biases = []
    for n in range(pps):
        blk = slice(n * HEADS, (n + 1) * HEADS)
        biases.append(jnp.concatenate([suf[blk] + run] * t_new, axis=0))
        run = run + (suf[blk][:, 0:1] + lf[blk][:, 0:1])
    run_ref[...] = run
    pair = lambda refs, n: jnp.concatenate([refs[n][0].astype(BF16), refs[n + 1][0].astype(BF16)], axis=1)
    scores = [_dot(qrows, pair(k_refs, n)) + jnp.concatenate(biases[n:n + 2], axis=1)
              for n in range(0, pps, 2)]
    m_old = m_ref[...]
    m_new = jnp.maximum(m_old, jnp.max(functools.reduce(jnp.maximum, scores), axis=-1, keepdims=True))
    alpha = jnp.exp(m_old - m_new)
    ps = [jnp.exp(s - m_new) for s in scores]
    pv = functools.reduce(lambda a, b: a + b,
                          [_dot_nt(p.astype(BF16), pair(v_refs, 2 * i)) for i, p in enumerate(ps)])
    m_ref[...] = m_new
    l_ref[...] = alpha * l_ref[...] + jnp.sum(functools.reduce(lambda a, b: a + b, ps), axis=-1, keepdims=True)
    acc_ref[...] = alpha * acc_ref[...] + pv

    @pl.when(step == pl.num_programs(1) - 1)
    def _():
        pad_ref[...] = jnp.zeros(pad_ref.shape, F32)
        pad_ref[0, 0:t_new, :] = kn_ref[0]
        pad_ref[1, 0:t_new, :] = vn_ref[0]
        pad_ref[2, 0:t_new, 0:LANES] = ckn_ref[0]
        kn = pad_ref[0].astype(BF16)
        vn = pad_ref[1].astype(BF16)
        ckn = pad_ref[2, :, 0:LANES].astype(BF16)
        r8 = lax.broadcasted_iota(jnp.int32, (n_rows, LANES), 0) & (HEADS - 1)
        l8 = lax.broadcasted_iota(jnp.int32, (n_rows, LANES), 1)
        sel = (lax.shift_right_logical(l8, 3) == r8) & ((l8 & (BIAS_STRIDE - 1)) >= BIAS_PARTS) \
            & ((l8 & (BIAS_STRIDE - 1)) < 2 * BIAS_PARTS)
        cqr = jnp.where(sel, 1.0, 0.0).astype(BF16)
        s = _dot_nt(jnp.concatenate([qrows, cqr], axis=1), jnp.concatenate([kn, ckn], axis=1))
        key = lax.broadcasted_iota(jnp.int32, (n_rows, page), 1)
        tok = lax.shift_right_logical(lax.broadcasted_iota(jnp.int32, (n_rows, page), 0), 3)
        s = jnp.where(key <= tok, s, -jnp.inf)
        _, l, acc = _softmax_step((m_ref[...], l_ref[...], acc_ref[...]), s, vn)
        o = jnp.where(head_mask, acc / l, 0.0)
        o_ref[0] = jnp.sum(o.reshape(t_new, HEADS, WIDTH), axis=1).astype(BF16)


def _fox_sample(q, kb, vb, ck, cache_k, cache_v, cache_lf_t, page_table):
    db, t_new, _ = q.shape
    n_pages = page_table.shape[1]
    n_pool, _, page = cache_k.shape
    pps = 16
    assert n_pages % pps == 0 and pps % 2 == 0 and page == LANES and t_new <= HEADS
    n_rows = t_new * HEADS
    pt = page_table.reshape(-1)

    def page_idx(n):
        return lambda b, s, pt_ref: (pt_ref[b * n_pages + n_pages - 1 - (s * pps + n)], 0, 0)

    new = lambda w: pl.BlockSpec((1, t_new, w), lambda b, s, pt_ref: (b, 0, 0))
    in_specs = ([new(WIDTH), new(WIDTH), new(WIDTH), new(LANES)]
                + [pl.BlockSpec((1, WIDTH, page), page_idx(n)) for n in range(pps)]
                + [pl.BlockSpec((1, WIDTH, page), page_idx(n)) for n in range(pps)]
                + [pl.BlockSpec((1, HEADS, page), page_idx(n)) for n in range(pps)])
    grid_spec = pltpu.PrefetchScalarGridSpec(
        num_scalar_prefetch=1,
        grid=(db, n_pages // pps),
        in_specs=in_specs,
        out_specs=new(WIDTH),
        scratch_shapes=[pltpu.VMEM((n_rows, WIDTH), BF16),
                        pltpu.VMEM((n_rows, 1), F32), pltpu.VMEM((n_rows, 1), F32),
                        pltpu.VMEM((n_rows, WIDTH), F32), pltpu.VMEM((HEADS, 1), F32),
                        pltpu.VMEM((3, page, WIDTH), F32)],
    )
    return pl.pallas_call(
        functools.partial(_fox_sample_kernel, t_new=t_new, pps=pps, page=page),
        out_shape=jax.ShapeDtypeStruct((db, t_new, WIDTH), BF16),
        grid_spec=grid_spec,
        compiler_params=_params(("arbitrary", "arbitrary")),
        name="fox_sample",
    )(pt, q, kb, vb, ck, *([cache_k] * pps), *([cache_v] * pps), *([cache_lf_t] * pps))


def _mlp_kernel(x_ref, or_ref, of_ref, gt_ref, lg_ref, lb_ref, wb_ref, wo_ref, g1_ref, b1_ref,
                wu_ref, wd_ref, g2_ref, b2_ref, y_ref, *, alpha, ff_chunk):
    d = x_ref.shape[1]
    hp = _layer_norm(x_ref[...], lg_ref[...], lb_ref[...])
    gt = gt_ref[...]
    merged = (gt[:, :d].astype(F32) * _dot(or_ref[...], wb_ref[0:WIDTH, :])
              + gt[:, d:].astype(F32) * _dot(of_ref[...], wb_ref[WIDTH:2 * WIDTH, :]))
    x1 = _layer_norm(alpha * hp + _dot(merged.astype(BF16), wo_ref[...]), g1_ref[...], b1_ref[...])
    x1b = x1.astype(BF16)
    h = jnp.zeros_like(x1)
    for c in range(wu_ref.shape[1] // ff_chunk):
        cs = slice(c * ff_chunk, (c + 1) * ff_chunk)
        up = jnp.maximum(_dot(x1b, wu_ref[:, cs]), 0.0)
        h = h + _dot((up * up).astype(BF16), wd_ref[cs, :])
    y_ref[...] = _layer_norm(alpha * x1 + h, g2_ref[...], b2_ref[...])


def _mlp(x2d, o_r, o_f, gates, ln_g, ln_b, wb, wo, g1, b1, wu, wd, g2, b2, alpha):
    m, d = x2d.shape
    tm = min(512, m)
    assert m % tm == 0
    row = lambda w: pl.BlockSpec((tm, w), lambda i: (i, 0))
    consts = (ln_g, ln_b, wb, wo, g1, b1, wu, wd, g2, b2)
    return pl.pallas_call(
        functools.partial(_mlp_kernel, alpha=alpha, ff_chunk=1024),
        out_shape=jax.ShapeDtypeStruct((m, d), F32),
        grid=(m // tm,),
        in_specs=[row(d), row(WIDTH), row(WIDTH), row(2 * d)] + [_const_spec(c.shape) for c in consts],
        out_specs=row(d),
        compiler_params=_params(("arbitrary",)),
        name="mlp",
    )(x2d, o_r, o_f, gates, *consts)


def kernel(x_prompt, x_sample, state_wkv, state_shift, cache_k, cache_v, cache_logf, page_table, ln_in_g, ln_in_b, w_in, rwkv_mu, rwkv_w0, rwkv_w2, rwkv_a0, rwkv_a2, rwkv_g2, rwkv_k_k, rwkv_k_a, rwkv_r_k, rwkv_gn_g, rwkv_gn_b, fox_b_f, w_branch, w_out, ln1_g, ln1_b, w_up, w_down, ln2_g, ln2_b):
    depth = w_in.shape[0]
    assert depth == 1, "the entry LayerNorm is fused into the first layer's kernels"
    bp, seq, d = x_prompt.shape
    db, dseq, _ = x_sample.shape
    n_pool, page = cache_k.shape[1], cache_k.shape[2]
    alpha = (2.0 * depth) ** 0.25
    row2 = lambda t: t.reshape(1, -1).astype(F32)
    ln_g, ln_b = row2(ln_in_g), row2(ln_in_b)

    l = 0
    w = w_in[l]
    fox_off = RWKV_PROJ
    wr = w[:, :RWKV_PROJ].astype(BF16)
    wq = w[:, fox_off:fox_off + WIDTH].astype(BF16)
    wt = w.T
    wkt = wt[fox_off + WIDTH:fox_off + 2 * WIDTH].astype(BF16)
    wvt = wt[fox_off + 2 * WIDTH:fox_off + 3 * WIDTH].astype(BF16)
    wft = jnp.pad(wt[fox_off + 3 * WIDTH:fox_off + 3 * WIDTH + HEADS], ((0, LANES - HEADS), (0, 0))).astype(BF16)
    wg = w[:, fox_off + 3 * WIDTH + HEADS:].astype(BF16)
    bf_col = jnp.pad(fox_b_f[l].reshape(HEADS, 1), ((0, LANES - HEADS), (0, 0))).astype(F32)
    proj_consts = (ln_g, ln_b, wq, wkt, wvt, wft, bf_col, wg)
    zl = jnp.zeros((DECAY_LORA, WIDTH), BF16)
    w2p = jnp.concatenate([rwkv_w2[l].astype(BF16), zl], axis=0)
    a2p = jnp.concatenate([zl, rwkv_a2[l].astype(BF16)], axis=0)
    rwkv_consts = (ln_g, ln_b, wr, row2(rwkv_mu[l]), row2(rwkv_w0[l]), w2p, row2(rwkv_a0[l]), a2p,
                   rwkv_g2[l].astype(BF16), row2(rwkv_k_k[l]), row2(rwkv_k_a[l]), row2(rwkv_r_k[l]),
                   row2(rwkv_gn_g[l]), row2(rwkv_gn_b[l]))
    mlp_consts = (ln_g, ln_b, w_branch[l].astype(BF16), w_out[l].astype(BF16), row2(ln1_g[l]),
                  row2(ln1_b[l]), w_up[l].astype(BF16), w_down[l].astype(BF16), row2(ln2_g[l]),
                  row2(ln2_b[l]))

    xp2 = x_prompt.reshape(bp * seq, d)
    q, kt32, vt32, ktb, vtb, lft, cq, ckt, gates = _proj(xp2, *proj_consts, seq, LOG2_E)
    to_seq = lambda t: t.reshape(bp, seq, t.shape[-1])
    o_f = _fox_prompt(to_seq(q), to_seq(cq), ktb, vtb, ckt)
    o_r, p_shift, p_wkv = _rwkv(x_prompt, seq, *rwkv_consts,
                                jnp.zeros((bp, RWKV_PROJ), F32),
                                jnp.zeros((bp, HEADS, HEAD_DIM, HEAD_DIM), F32))
    y_prompt = _mlp(xp2, o_r.reshape(bp * seq, WIDTH), o_f.reshape(bp * seq, WIDTH), gates,
                    *mlp_consts, alpha).reshape(bp, seq, d)
    heads_last = lambda t: jnp.transpose(t.reshape(bp, HEADS, HEAD_DIM, seq), (0, 3, 1, 2))[None]
    p_k = heads_last(kt32)
    p_v = heads_last(vt32)
    p_lf = jnp.swapaxes(lft, 1, 2)[None]

    xs2 = x_sample.reshape(db * dseq, d)
    q, kt32, vt32, _, _, lft, cq, ckt, gates = _proj(xs2, *proj_consts, dseq, 1.0)
    rows = lambda t: jnp.swapaxes(t, 1, 2).reshape(db * dseq, t.shape[1])
    k32, v32, lf, ck = rows(kt32), rows(vt32), rows(lft), rows(ckt)
    to_seq = lambda t: t.reshape(db, dseq, t.shape[-1])
    o_f = _fox_sample(to_seq(q.astype(F32)), to_seq(k32), to_seq(v32), to_seq(ck.astype(F32)),
                      jnp.transpose(cache_k[l], (0, 2, 3, 1)).reshape(n_pool, WIDTH, page),
                      jnp.transpose(cache_v[l], (0, 2, 3, 1)).reshape(n_pool, WIDTH, page),
                      jnp.swapaxes(cache_logf[l], 1, 2), page_table)
    xs_pad = jnp.pad(x_sample, ((0, 0), (0, CHUNK - dseq), (0, 0)))
    o_r, s_shift, s_wkv = _rwkv(xs_pad, dseq, *rwkv_consts, state_shift[l], state_wkv[l])
    y_sample = _mlp(xs2, o_r[:, :dseq].reshape(db * dseq, WIDTH), o_f.reshape(db * dseq, WIDTH), gates,
                    *mlp_consts, alpha).reshape(db, dseq, d)
    s_k = k32.reshape(1, db, dseq, HEADS, HEAD_DIM)
    s_v = v32.reshape(1, db, dseq, HEADS, HEAD_DIM)
    s_lf = lf.reshape(1, db, dseq, HEADS)

    return (y_prompt, y_sample, p_wkv[None], p_shift[None], p_k, p_v, p_lf,
            s_wkv[None], s_shift[None], s_k, s_v, s_lf)
```

```python
import functools
import math

import numpy as np
import jax
import jax.numpy as jnp
from jax import lax
from jax.experimental import pallas as pl
from jax.experimental.pallas import tpu as pltpu

F32 = jnp.float32
BF16 = jnp.bfloat16

HEADS = 8
HEAD_DIM = 64
WIDTH = HEADS * HEAD_DIM
DECAY_LORA = 64
AAA_LORA = 64
GATE_LORA = 128
RWKV_PROJ = 3 * WIDTH + DECAY_LORA + AAA_LORA + GATE_LORA
LORA_OFF = 3 * WIDTH
GATE_LORA_OFF = LORA_OFF + DECAY_LORA + AAA_LORA
GN_EPS = 64e-5
LN_EPS = 1e-5
QK_SCALE = HEAD_DIM ** -0.5
LOG2_E = math.log2(math.e)

LANES = 128
KV_BLOCK = 256
QUERY_BLOCK = 1024
CHUNK = 64
GROUP_HEADS = 4
GROUP_W = GROUP_HEADS * HEAD_DIM
N_GROUPS = HEADS // GROUP_HEADS
BIAS_PARTS = 3
BIAS_STRIDE = 8
VMEM_LIMIT = 56 * 1024 * 1024


def _dot(a, b):
    return jnp.dot(a, b, preferred_element_type=F32)


def _dot_nt(a, b):
    return lax.dot_general(a, b, (((1,), (1,)), ((), ())), preferred_element_type=F32)


def _dot_tn(a, b):
    return lax.dot_general(a, b, (((0,), (0,)), ((), ())), preferred_element_type=F32)


def _layer_norm(x, g, b):
    mu = jnp.mean(x, -1, keepdims=True)
    xc = x - mu
    var = jnp.mean(xc * xc, -1, keepdims=True)
    return xc * lax.rsqrt(var + LN_EPS) * g + b


def _split3(x):
    p1 = x.astype(BF16)
    r1 = x - p1.astype(F32)
    p2 = r1.astype(BF16)
    p3 = (r1 - p2.astype(F32)).astype(BF16)
    return p1, p2, p3


def _dot3(m_bf16, x):
    p1, p2, p3 = _split3(x)
    return _dot(m_bf16, p1) + _dot(m_bf16, p2) + _dot(m_bf16, p3)


def _dot3_rhs(x, m_bf16):
    p1, p2, p3 = _split3(x)
    return _dot(p1, m_bf16) + _dot(p2, m_bf16) + _dot(p3, m_bf16)


def _seq_tri(n, seq_len, upper=False):
    row = lax.broadcasted_iota(jnp.int32, (n, n), 0)
    col = lax.broadcasted_iota(jnp.int32, (n, n), 1)
    keep = (row <= col) if upper else (col <= row)
    if seq_len < n:
        sh = int(math.log2(seq_len))
        assert 1 << sh == seq_len
        keep = keep & (lax.shift_right_logical(row, sh) == lax.shift_right_logical(col, sh))
    return jnp.where(keep, 1.0, 0.0).astype(BF16)


def _const_spec(shape):
    nd = len(shape)
    return pl.BlockSpec(shape, lambda *_: (0,) * nd, pipeline_mode=pl.Buffered(1))


def _params(sem):
    return pltpu.CompilerParams(dimension_semantics=sem, vmem_limit_bytes=VMEM_LIMIT)


def _bias_consts():
    e = np.zeros((2 * BIAS_PARTS, LANES, LANES), np.float32)
    ones_q = np.zeros((1, LANES), np.float32)
    ones_k = np.zeros((1, LANES), np.float32)
    for h in range(HEADS):
        for j in range(BIAS_PARTS):
            e[j, h, BIAS_STRIDE * h + j] = 1.0
            e[BIAS_PARTS + j, h, BIAS_STRIDE * h + BIAS_PARTS + j] = 1.0
            ones_q[0, BIAS_STRIDE * h + BIAS_PARTS + j] = 1.0
            ones_k[0, BIAS_STRIDE * h + j] = 1.0
    return jnp.asarray(e, BF16), jnp.asarray(ones_q), jnp.asarray(ones_k)


def _proj_kernel(x_ref, g_ref, b_ref, wq_ref, wkt_ref, wvt_ref, wft_ref, bf_ref, wg_ref,
                 e_ref, et_ref, oq_ref, ok_ref,
                 q_ref, kt_ref, vt_ref, ktb_ref, vtb_ref, lft_ref, cq_ref, ckt_ref, gt_ref,
                 carry_ref, *, seq_len, tm, logit_scale):
    xn = _layer_norm(x_ref[...], g_ref[...], b_ref[...]).astype(BF16)
    q_ref[...] = (_dot(xn, wq_ref[...]) * (QK_SCALE * logit_scale)).astype(BF16)
    gt_ref[...] = jax.nn.sigmoid(_dot(xn, wg_ref[...])).astype(BF16)
    for w_ref, o32_ref, ob_ref in ((wkt_ref, kt_ref, ktb_ref), (wvt_ref, vt_ref, vtb_ref)):
        t = _dot_nt(w_ref[...], xn)
        o32_ref[0] = t
        for jb in range(tm // KV_BLOCK):
            ob_ref[jb] = t[:, jb * KV_BLOCK:(jb + 1) * KV_BLOCK].astype(BF16)

    rowi = lax.broadcasted_iota(jnp.int32, (LANES, tm), 0)
    lft = jax.nn.log_sigmoid(_dot_nt(wft_ref[...], xn) + bf_ref[...])
    lft = jnp.where(rowi < HEADS, lft, 0.0)
    lft_ref[0] = lft[:HEADS, :]

    ct = _dot3_rhs(lft, _seq_tri(tm, seq_len, upper=True))
    if seq_len > tm:
        @pl.when(pl.program_id(0) % (seq_len // tm) == 0)
        def _():
            carry_ref[...] = jnp.zeros_like(carry_ref)
        ct = ct + carry_ref[...]
        carry_ref[...] = ct[:, tm - 1:tm]

    if logit_scale != 1.0:
        ct = ct * logit_scale
    c1t, c2t, c3t = _split3(ct)
    ckt = ok_ref[...] - (_dot(et_ref[0], c1t) + _dot(et_ref[1], c2t) + _dot(et_ref[2], c3t))
    for jb in range(tm // KV_BLOCK):
        ckt_ref[jb] = ckt[:, jb * KV_BLOCK:(jb + 1) * KV_BLOCK].astype(BF16)
    c1, c2, c3 = _split3(ct.T)
    cq = _dot(c1, e_ref[0]) + _dot(c2, e_ref[1]) + _dot(c3, e_ref[2]) + oq_ref[...]
    cq_ref[...] = cq.astype(BF16)


def _proj(x2d, ln_g, ln_b, wq, wkt, wvt, wft, bf_col, wg, seq_len, logit_scale):
    m, d = x2d.shape
    tm = min(512, m)
    assert m % tm == 0 and (seq_len % tm == 0 or tm % seq_len == 0) and tm % KV_BLOCK == 0
    e, ones_q, ones_k = _bias_consts()
    e_q = e[:BIAS_PARTS]
    e_kt = jnp.swapaxes(e[BIAS_PARTS:], 1, 2)
    ones_k_col = ones_k.reshape(LANES, 1)
    if seq_len >= tm:
        tps = seq_len // tm
        groups, cols = m // seq_len, seq_len
        tr = lambda r: pl.BlockSpec((1, r, tm), lambda i: (i // tps, 0, i % tps))
    else:
        groups, cols = m // tm, tm
        tr = lambda r: pl.BlockSpec((1, r, tm), lambda i: (i, 0, 0))
    nkb = tm // KV_BLOCK
    trb = lambda r: pl.BlockSpec((nkb, r, KV_BLOCK), lambda i: (i, 0, 0))
    row = lambda w: pl.BlockSpec((tm, w), lambda i: (i, 0))
    out_shape = (
        jax.ShapeDtypeStruct((m, WIDTH), BF16),
        jax.ShapeDtypeStruct((groups, WIDTH, cols), F32),
        jax.ShapeDtypeStruct((groups, WIDTH, cols), F32),
        jax.ShapeDtypeStruct((m // KV_BLOCK, WIDTH, KV_BLOCK), BF16),
        jax.ShapeDtypeStruct((m // KV_BLOCK, WIDTH, KV_BLOCK), BF16),
        jax.ShapeDtypeStruct((groups, HEADS, cols), F32),
        jax.ShapeDtypeStruct((m, LANES), BF16),
        jax.ShapeDtypeStruct((m // KV_BLOCK, LANES, KV_BLOCK), BF16),
        jax.ShapeDtypeStruct((m, 2 * d), BF16),
    )
    consts = (ln_g, ln_b, wq, wkt, wvt, wft, bf_col, wg, e_q, e_kt, ones_q, ones_k_col)
    return pl.pallas_call(
        functools.partial(_proj_kernel, seq_len=seq_len, tm=tm, logit_scale=logit_scale),
        out_shape=out_shape,
        grid=(m // tm,),
        in_specs=[row(d)] + [_const_spec(c.shape) for c in consts],
        out_specs=(row(WIDTH), tr(WIDTH), tr(WIDTH), trb(WIDTH), trb(WIDTH), tr(HEADS),
                   row(LANES), trb(LANES), row(2 * d)),
        scratch_shapes=[pltpu.VMEM((LANES, 1), F32)],
        compiler_params=_params(("arbitrary",)),
        name="proj",
    )(x2d, *consts)


def _group_mask(rows):
    r = lax.broadcasted_iota(jnp.int32, (rows, GROUP_W), 0)
    c = lax.broadcasted_iota(jnp.int32, (rows, GROUP_W), 1)
    sh = int(math.log2(HEAD_DIM))
    return (lax.shift_right_logical(r, sh) & (GROUP_HEADS - 1)) == lax.shift_right_logical(c, sh)


def _bd(y, mask):
    yb = y.astype(BF16)
    return jnp.where(mask, jnp.concatenate([yb] * GROUP_HEADS, axis=0), jnp.zeros((), BF16))


def _state_to_bd(st, mask_bd):
    out = []
    for g in range(N_GROUPS):
        blk = st[g * GROUP_W:(g + 1) * GROUP_W, :]
        out.append(jnp.where(mask_bd, jnp.concatenate([blk] * GROUP_HEADS, axis=1), 0.0))
    return out


def _state_from_bd(state, mask_bd):
    rows = []
    for sg in state:
        sm = jnp.where(mask_bd, sg, 0.0)
        acc = sm[:, 0:HEAD_DIM]
        for h in range(1, GROUP_HEADS):
            acc = acc + sm[:, h * HEAD_DIM:(h + 1) * HEAD_DIM]
        rows.append(acc)
    return jnp.concatenate(rows, axis=0)


def _rwkv_kernel(x_ref, g_ref, b_ref, wr_ref, mu_ref, w0_ref, w2_ref, a0_ref, a2_ref, g2_ref,
                 kk_ref, ka_ref, rk_ref, gng_ref, gnb_ref, ones_ref, shift_ref, state_ref,
                 o_ref, shift_out_ref, state_out_ref,
                 s_ref, prev_ref, *, tm, t_real, chained):
    n_sub = 2 if tm // CHUNK >= 4 else 1
    rows = tm // n_sub
    nc = rows // CHUNK
    mask_bd = _group_mask(GROUP_W)
    rowi = lax.broadcasted_iota(jnp.int32, (rows, 1), 0)
    tri3 = jnp.concatenate([_seq_tri(CHUNK, CHUNK)] * BIAS_PARTS, axis=1)
    last_rows = {}
    state_written = set()

    if chained:
        @pl.when(pl.program_id(1) == 0)
        def _():
            prev_ref[...] = shift_ref[0]
            for g, sg in enumerate(_state_to_bd(state_ref[0], mask_bd)):
                s_ref[g] = sg

    def prologue(si):
        r0, c0 = si * rows, si * nc
        x = x_ref[...].reshape(tm, x_ref.shape[-1])[r0:r0 + rows]
        xn = _layer_norm(x, g_ref[...], b_ref[...]).astype(BF16)
        p = _dot(xn, wr_ref[...])
        last_rows[si] = p[rows - 1:rows, :]
        yield
        if chained:
            prev = prev_ref[...] if si == 0 else last_rows[si - 1]
            p_prev = jnp.where(rowi == 0, prev, pltpu.roll(p, 1, 0))
            if si == n_sub - 1:
                prev_ref[...] = p[rows - 1:rows, :]
                shift_out_ref[0] = p[rows - 1:rows, :]
        else:
            first = jnp.concatenate([jnp.broadcast_to(shift_ref[c0 + c], (CHUNK, RWKV_PROJ)) for c in range(nc)],
                                    axis=0)
            p_prev = jnp.where((rowi & (CHUNK - 1)) == 0, first, pltpu.roll(p, 1, 0))
            for c in range(nc):
                shift_out_ref[c0 + c] = p[c * CHUNK + t_real - 1:c * CHUNK + t_real, :]
        pm = p + (p_prev - p) * mu_ref[...]
        r = pm[:, 0:WIDTH]
        k = pm[:, WIDTH:2 * WIDTH]
        v = pm[:, 2 * WIDTH:3 * WIDTH]
        dwa = pm[:, LORA_OFF:GATE_LORA_OFF]
        dg = pm[:, GATE_LORA_OFF:RWKV_PROJ]
        w_log = -jax.nn.softplus(-(w0_ref[...] + _dot(jnp.tanh(dwa).astype(BF16), w2_ref[...]))) - 0.5
        lw = -jnp.exp(w_log)
        a = jax.nn.sigmoid(a0_ref[...] + _dot(dwa.astype(BF16), a2_ref[...]))
        gate = _dot(jax.nn.sigmoid(dg).astype(BF16), g2_ref[...])
        yield
        kk = k * kk_ref[...]
        ss = _dot((kk * kk).astype(BF16), ones_ref[...])
        kk = kk * lax.rsqrt(jnp.maximum(ss, 1e-24))
        k = k * (1.0 + (a - 1.0) * ka_ref[...])
        bonus = _dot((r * k * rk_ref[...]).astype(BF16), ones_ref[...]) * v
        kka = kk * a
        if not chained and t_real < CHUNK:
            live = (rowi & (CHUNK - 1)) < t_real
            lw = jnp.where(live, lw, 0.0)
            r, k, v, kk, kka = (jnp.where(live, t, 0.0) for t in (r, k, v, kk, kka))
        yield
        lw_parts = _split3(lw)
        cl = jnp.concatenate(
            [_dot(tri3, jnp.concatenate([t[c * CHUNK:(c + 1) * CHUNK, :] for t in lw_parts], axis=0))
             for c in range(nc)], axis=0) if nc > 1 else _dot(tri3, jnp.concatenate(lw_parts, axis=0))
        e_ng = jnp.exp(-cl)
        return dict(rt=r * jnp.exp(cl), at=-kk * jnp.exp(cl - lw), bt=kka * e_ng, kt=k * e_ng,
                    v=v, k=k, kka=kka, cl=cl, bonus=bonus, gate=gate)

    row = lax.broadcasted_iota(jnp.int32, (CHUNK, GROUP_W), 0)
    col = lax.broadcasted_iota(jnp.int32, (CHUNK, GROUP_W), 1) & (CHUNK - 1)
    strict = col < row
    incl = col <= row
    blk16 = lax.shift_right_logical(row, 4) == lax.shift_right_logical(col, 4)
    blk32 = lax.shift_right_logical(row, 5) == lax.shift_right_logical(col, 5)

    def mm(x, y):
        return _dot(x.astype(BF16), _bd(y, mask_bd))

    probs = [(c, g) for c in range(nc) for g in range(N_GROUPS)]
    sl = lambda x, c, g: x[c * CHUNK:(c + 1) * CHUNK, g * GROUP_W:(g + 1) * GROUP_W]
    each = lambda f, *ls: [f(*xs) for xs in zip(*ls)]
    add = lambda *xs: functools.reduce(lambda a_, b_: a_ + b_, xs)
    stack = lambda a_, b_: jnp.concatenate([a_, b_], axis=0)
    top = lambda xs: [x[:CHUNK] for x in xs]
    bot = lambda xs: [x[CHUNK:] for x in xs]

    def scan(si, d):
        rt, at, bt, kt, v, k, kka, cl = (d[n] for n in ("rt", "at", "bt", "kt", "v", "k", "kka", "cl"))
        rt_, at_, bt_, kt_, v_ = ([sl(x, c, g) for c, g in probs] for x in (rt, at, bt, kt, v))
        cl_end = [cl[(c + 1) * CHUNK - 1:(c + 1) * CHUNK, :] for c in range(nc)]
        g_end = [jnp.exp(ce) for ce in cl_end]
        e_end = [jnp.exp(cl_end[c] - cl[c * CHUNK:(c + 1) * CHUNK, :]) for c in range(nc)]
        bh_ = [sl(kka, c, g) * e_end[c][:, g * GROUP_W:(g + 1) * GROUP_W] for c, g in probs]
        kh_ = [sl(k, c, g) * e_end[c][:, g * GROUP_W:(g + 1) * GROUP_W] for c, g in probs]
        yield
        ar_ = each(lambda a_, r_: jnp.concatenate([a_, r_], axis=0).astype(BF16), at_, rt_)
        gb_ = each(lambda ar, b_: _dot_nt(ar, _bd(b_, mask_bd)), ar_, bt_)
        yield
        gk_ = each(lambda ar, k_: _dot_nt(ar, _bd(k_, mask_bd)), ar_, kt_)
        n_ab = [jnp.where(strict, x[:CHUNK], 0.0) for x in gb_]
        a_ak = [jnp.where(strict, x[:CHUNK], 0.0) for x in gk_]
        p_rb = [jnp.where(incl, x[CHUNK:], 0.0) for x in gb_]
        p_rk = [jnp.where(incl, x[CHUNK:], 0.0) for x in gk_]
        yield
        uvp = each(mm, each(stack, a_ak, p_rk), v_)
        uv_, prkv = top(uvp), bot(uvp)
        yield
        n_d = [jnp.where(blk16, x, 0.0) for x in n_ab]
        n_1 = [jnp.where(blk32 & jnp.logical_not(blk16), x, 0.0) for x in n_ab]
        n_2 = [jnp.where(blk32, 0.0, x) for x in n_ab]
        n2 = each(mm, n_d, n_d)
        yield
        n34 = each(mm, each(stack, n_d, n2), n2)
        n3, n4 = top(n34), bot(n34)
        q4 = each(add, n_d, n2, n3)
        yield
        m48 = each(mm, each(stack, q4, n4), n4)
        q8 = each(add, q4, n4, top(m48))
        n8 = bot(m48)
        yield
        t16 = each(add, q8, n8, each(mm, q8, n8))
        yield
        x1 = each(add, n_1, each(mm, n_1, t16))
        yield
        t32 = each(add, t16, x1, each(mm, t16, x1))
        yield
        x2 = each(add, n_2, each(mm, n_2, t32))
        yield
        t_ = each(add, t32, x2, each(mm, t32, x2))
        yield
        wt_ = each(add, at_, each(mm, t_, at_))
        yield
        ut_ = each(add, uv_, each(mm, t_, uv_))
        yield
        mw_ = each(lambda w_, b_: jnp.where(mask_bd, _dot_tn(w_.astype(BF16), b_.astype(BF16)), 0.0).astype(BF16),
                   wt_, bh_)
        yield
        dd_ = each(lambda u_, vv, b_, k_: jnp.where(mask_bd, _dot_tn(
            jnp.concatenate([u_, vv], axis=0).astype(BF16), jnp.concatenate([b_, k_], axis=0).astype(BF16)), 0.0),
                   ut_, v_, bh_, kh_)
        yield
        qq_ = each(lambda r_, x: (r_ + x).astype(BF16), rt_, each(mm, p_rb, wt_))
        yield
        o0_ = each(add, each(mm, p_rb, ut_), prkv)
        yield

        if chained:
            while si > 0 and (si - 1) not in state_written:
                yield
            state = [s_ref[g] for g in range(N_GROUPS)]
        o_chunks = []
        for c in range(nc):
            if not chained:
                state = _state_to_bd(state_ref[si * nc + c], mask_bd)
            o_groups = []
            for g in range(N_GROUPS):
                j = c * N_GROUPS + g
                s0 = state[g]
                s0b = s0.astype(BF16)
                o_groups.append(_dot_nt(qq_[j], s0b) + o0_[j])
                state[g] = s0 * g_end[c][:, g * GROUP_W:(g + 1) * GROUP_W] + _dot(s0b, mw_[j]) + dd_[j]
            o_chunks.append(jnp.concatenate(o_groups, axis=1))
            if not chained:
                state_out_ref[si * nc + c] = _state_from_bd(state, mask_bd)
            yield
        if chained:
            for g in range(N_GROUPS):
                s_ref[g] = state[g]
            state_written.add(si)
        o = jnp.concatenate(o_chunks, axis=0) if len(o_chunks) > 1 else o_chunks[0]

        inv_n = 1.0 / HEAD_DIM
        mean = _dot(o.astype(BF16), ones_ref[...]) * inv_n
        oc = o - mean
        var = _dot((oc * oc).astype(BF16), ones_ref[...]) * inv_n
        on = oc * lax.rsqrt(var + GN_EPS) * gng_ref[...] + gnb_ref[...]
        out = ((on + d["bonus"]) * d["gate"]).astype(BF16)
        if chained:
            o_ref[0, si * rows:(si + 1) * rows, :] = out
        else:
            o_ref[si * nc:(si + 1) * nc] = out.reshape(nc, CHUNK, WIDTH)

    def pipeline(si):
        d = yield from prologue(si)
        yield "prologue done"
        yield from scan(si, d)

    gens = [pipeline(si) for si in range(n_sub)]
    started, prologue_done, live = 0, set(), []
    while started < n_sub or live:
        if started < n_sub and (started == 0 or (started - 1) in prologue_done):
            live.append(started)
            started += 1
        for si in list(live):
            try:
                if next(gens[si]) == "prologue done":
                    prologue_done.add(si)
            except StopIteration:
                live.remove(si)

    if chained:
        @pl.when(pl.program_id(1) == pl.num_programs(1) - 1)
        def _():
            state_out_ref[0] = _state_from_bd([s_ref[g] for g in range(N_GROUPS)], mask_bd)


def _rwkv(x3d, t_real, ln_g, ln_b, wr, mu, w0, w2p, a0, a2p, g2, k_k, k_a, r_k, gn_g, gn_b,
          shift_in, state_in):
    nseq, t, d = x3d.shape
    chained = t_real == t
    if chained:
        tm, nb = min(512, t), 1
        assert t % tm == 0 and tm % CHUNK == 0
        grid = (nseq, t // tm)
        x_spec = pl.BlockSpec((1, tm, d), lambda b, i: (b, i, 0))
        o_spec = pl.BlockSpec((1, tm, WIDTH), lambda b, i: (b, i, 0))
    else:
        nb = min(8, nseq)
        tm = nb * CHUNK
        assert t == CHUNK and nseq % nb == 0
        grid = (nseq // nb, 1)
        x_spec = pl.BlockSpec((nb, CHUNK, d), lambda b, i: (b, 0, 0))
        o_spec = pl.BlockSpec((nb, CHUNK, WIDTH), lambda b, i: (b, 0, 0))
    ones_bd = jnp.asarray(np.kron(np.eye(HEADS, dtype=np.float32),
                                  np.ones((HEAD_DIM, HEAD_DIM), np.float32)), BF16)
    consts = (ln_g, ln_b, wr, mu, w0, w2p, a0, a2p, g2, k_k, k_a, r_k, gn_g, gn_b, ones_bd)
    state2d = state_in.reshape(nseq, HEADS * HEAD_DIM, HEAD_DIM)
    shift3d = shift_in.reshape(nseq, 1, RWKV_PROJ)
    seq_spec = lambda shape: pl.BlockSpec((nb,) + shape, lambda b, i: (b, 0, 0))
    o, shift_out, state_out = pl.pallas_call(
        functools.partial(_rwkv_kernel, tm=tm, t_real=t_real, chained=chained),
        out_shape=(jax.ShapeDtypeStruct((nseq, t, WIDTH), BF16),
                   jax.ShapeDtypeStruct((nseq, 1, RWKV_PROJ), F32),
                   jax.ShapeDtypeStruct((nseq, HEADS * HEAD_DIM, HEAD_DIM), F32)),
        grid=grid,
        in_specs=[x_spec] + [_const_spec(c.shape) for c in consts]
                 + [seq_spec((1, RWKV_PROJ)), seq_spec((HEADS * HEAD_DIM, HEAD_DIM))],
        out_specs=(o_spec, seq_spec((1, RWKV_PROJ)), seq_spec((HEADS * HEAD_DIM, HEAD_DIM))),
        scratch_shapes=[pltpu.VMEM((N_GROUPS, GROUP_W, GROUP_W), F32),
                        pltpu.VMEM((1, RWKV_PROJ), F32)],
        compiler_params=_params(("arbitrary", "arbitrary")),
        name="rwkv",
    )(x3d, *consts, shift3d, state2d)
    return (o, shift_out.reshape(nseq, RWKV_PROJ),
            state_out.reshape(nseq, HEADS, HEAD_DIM, HEAD_DIM))


def _softmax_step(carry, s, v_blk):
    m, l, acc = carry
    m_new = jnp.maximum(m, jnp.max(s, axis=-1, keepdims=True))
    alpha = jnp.exp(m - m_new)
    p = jnp.exp(s - m_new)
    l = alpha * l + jnp.sum(p, axis=-1, keepdims=True)
    acc = alpha * acc + _dot(p.astype(BF16), v_blk)
    return m_new, l, acc


def _fox_prompt_kernel(q_ref, cq_ref, k_ref, v_ref, ck_ref, o_ref, qa_ref, m_ref, acc_ref, *, tq):
    i = pl.program_id(1)
    lane = lax.broadcasted_iota(jnp.int32, (tq, LANES), 1)
    cq = cq_ref[0]
    zero = jnp.zeros((), BF16)
    for h in range(HEADS):
        q_pair = q_ref[0, :, (h // 2) * LANES:(h // 2 + 1) * LANES]
        q_h = jnp.where(lax.shift_right_logical(lane, 6) == (h % 2), q_pair, zero)
        c_h = jnp.where(lax.shift_right_logical(lane, 3) == h, cq, zero)
        qa_ref[h] = jnp.concatenate([q_h, c_h], axis=1)
    m_ref[...] = jnp.full(m_ref.shape, -jnp.inf, F32)
    acc_ref[...] = jnp.zeros(acc_ref.shape, F32)
    tk = KV_BLOCK
    ones = jnp.ones((LANES, tk), BF16)

    def block(j, r0, mask):
        ck = ck_ref[j]
        for pair in range(HEADS // 2):
            ls = slice(pair * LANES, (pair + 1) * LANES)
            ka = jnp.concatenate([k_ref[j, ls, :], ck], axis=0)
            v_aug = jnp.concatenate([v_ref[j, ls, :], ones], axis=0)
            for h in (2 * pair, 2 * pair + 1):
                s = _dot(qa_ref[h, r0:, :], ka)
                if mask is not None:
                    s = jnp.where(mask, s, -jnp.inf)
                parts = [s[:, t * LANES:(t + 1) * LANES] for t in range(tk // LANES)]
                m_old = m_ref[h, r0:, :]
                m_new = jnp.maximum(m_old, jnp.max(functools.reduce(jnp.maximum, parts), axis=-1, keepdims=True))
                alpha = jnp.exp2(m_old - m_new)
                p = jnp.concatenate([jnp.exp2(t - m_new) for t in parts], axis=1).astype(BF16)
                m_ref[h, r0:, :] = m_new
                acc_ref[h, r0:, :] = (jnp.concatenate([alpha, alpha], axis=1) * acc_ref[h, r0:, :]
                                      + _dot_nt(p, v_aug))

    def body(j, carry):
        block(j, 0, None)
        return carry

    n_full = i * (tq // tk)
    lax.fori_loop(0, n_full, body, 0)
    for d in range(tq // tk):
        rows = tq - d * tk
        row = lax.broadcasted_iota(jnp.int32, (rows, tk), 0)
        col = lax.broadcasted_iota(jnp.int32, (rows, tk), 1)
        block(n_full + d, d * tk, col <= row)
    for pair in range(HEADS // 2):
        lo = acc_ref[2 * pair, :, 0:LANES] / acc_ref[2 * pair, :, LANES:2 * LANES]
        hi = acc_ref[2 * pair + 1, :, 0:LANES] / acc_ref[2 * pair + 1, :, LANES:2 * LANES]
        o_ref[0, :, pair * LANES:(pair + 1) * LANES] = jnp.where(
            lax.shift_right_logical(lane, 6) == 0, lo, hi).astype(BF16)


def _fox_prompt(q, cq, ktb, vtb, ckt):
    b, s, _ = q.shape
    tq = min(QUERY_BLOCK, s)
    assert s % tq == 0 and tq % KV_BLOCK == 0
    nkb = s // KV_BLOCK
    blk = lambda w: pl.BlockSpec((1, tq, w), lambda bi, i: (bi, i, 0))
    full = lambda r: pl.BlockSpec((nkb, r, KV_BLOCK), lambda bi, i: (bi, 0, 0))
    return pl.pallas_call(
        functools.partial(_fox_prompt_kernel, tq=tq),
        out_shape=jax.ShapeDtypeStruct((b, s, WIDTH), BF16),
        grid=(b, s // tq),
        in_specs=[blk(WIDTH), blk(LANES), full(WIDTH), full(WIDTH), full(LANES)],
        out_specs=blk(WIDTH),
        scratch_shapes=[pltpu.VMEM((HEADS, tq, 2 * LANES), BF16),
                        pltpu.VMEM((HEADS, tq, LANES), F32),
                        pltpu.VMEM((HEADS, tq, 2 * LANES), F32)],
        compiler_params=_params(("arbitrary", "arbitrary")),
        name="fox_prompt",
    )(q, cq, ktb, vtb, ckt)


def _fox_sample_kernel(pt_ref, q_ref, kn_ref, vn_ref, ckn_ref, *refs, t_new, pps, page):
    k_refs = refs[0:pps]
    v_refs = refs[pps:2 * pps]
    lf_refs = refs[2 * pps:3 * pps]
    o_ref = refs[3 * pps]
    qrow_ref, m_ref, l_ref, acc_ref, run_ref, pad_ref = refs[3 * pps + 1:]
    del pt_ref
    step = pl.program_id(1)
    n_rows = t_new * HEADS
    rowh = lax.broadcasted_iota(jnp.int32, (n_rows, WIDTH), 0) & (HEADS - 1)
    laneh = lax.shift_right_logical(lax.broadcasted_iota(jnp.int32, (n_rows, WIDTH), 1), 6)
    head_mask = rowh == laneh

    @pl.when(step == 0)
    def _():
        q = q_ref[0]
        rows = jnp.concatenate([jnp.broadcast_to(q[t:t + 1, :], (HEADS, WIDTH)) for t in range(t_new)],
                               axis=0)
        qrow_ref[...] = jnp.where(head_mask, rows, 0.0).astype(BF16)
        m_ref[...] = jnp.full(m_ref.shape, -jnp.inf, F32)
        l_ref[...] = jnp.zeros(l_ref.shape, F32)
        acc_ref[...] = jnp.zeros(acc_ref.shape, F32)
        run_ref[...] = jnp.zeros(run_ref.shape, F32)

    qrows = qrow_ref[...]
    lf = jnp.concatenate([r[0] for r in lf_refs], axis=0)
    newer = lax.broadcasted_iota(jnp.int32, (page, page), 0) > lax.broadcasted_iota(jnp.int32, (page, page), 1)
    suf = _dot3_rhs(lf, jnp.where(newer, 1.0, 0.0).astype(BF16))
    run = run_ref[...]
    biases = []
    for n in range(pps):
        blk = slice(n * HEADS, (n + 1) * HEADS)
        biases.append(jnp.concatenate([suf[blk] + run] * t_new, axis=0))
        run = run + (suf[blk][:, 0:1] + lf[blk][:, 0:1])
    run_ref[...] = run
    pair = lambda refs, n: jnp.concatenate([refs[n][0].astype(BF16), refs[n + 1][0].astype(BF16)], axis=1)
    scores = [_dot(qrows, pair(k_refs, n)) + jnp.concatenate(biases[n:n + 2], axis=1)
              for n in range(0, pps, 2)]
    m_old = m_ref[...]
    m_new = jnp.maximum(m_old, jnp.max(functools.reduce(jnp.maximum, scores), axis=-1, keepdims=True))
    alpha = jnp.exp(m_old - m_new)
    ps = [jnp.exp(s - m_new) for s in scores]
    pv = functools.reduce(lambda a, b: a + b,
                          [_dot_nt(p.astype(BF16), pair(v_refs, 2 * i)) for i, p in enumerate(ps)])
    m_ref[...] = m_new
    l_ref[...] = alpha * l_ref[...] + jnp.sum(functools.reduce(lambda a, b: a + b, ps), axis=-1, keepdims=True)
    acc_ref[...] = alpha * acc_ref[...] + pv

    @pl.when(step == pl.num_programs(1) - 1)
    def _():
        pad_ref[...] = jnp.zeros(pad_ref.shape, F32)
        pad_ref[0, 0:t_new, :] = kn_ref[0]
        pad_ref[1, 0:t_new, :] = vn_ref[0]
        pad_ref[2, 0:t_new, 0:LANES] = ckn_ref[0]
        kn = pad_ref[0].astype(BF16)
        vn = pad_ref[1].astype(BF16)
        ckn = pad_ref[2, :, 0:LANES].astype(BF16)
        r8 = lax.broadcasted_iota(jnp.int32, (n_rows, LANES), 0) & (HEADS - 1)
        l8 = lax.broadcasted_iota(jnp.int32, (n_rows, LANES), 1)
        sel = (lax.shift_right_logical(l8, 3) == r8) & ((l8 & (BIAS_STRIDE - 1)) >= BIAS_PARTS) \
            & ((l8 & (BIAS_STRIDE - 1)) < 2 * BIAS_PARTS)
        cqr = jnp.where(sel, 1.0, 0.0).astype(BF16)
        s = _dot_nt(jnp.concatenate([qrows, cqr], axis=1), jnp.concatenate([kn, ckn], axis=1))
        key = lax.broadcasted_iota(jnp.int32, (n_rows, page), 1)
        tok = lax.shift_right_logical(lax.broadcasted_iota(jnp.int32, (n_rows, page), 0), 3)
        s = jnp.where(key <= tok, s, -jnp.inf)
        _, l, acc = _softmax_step((m_ref[...], l_ref[...], acc_ref[...]), s, vn)
        o = jnp.where(head_mask, acc / l, 0.0)
        o_ref[0] = jnp.sum(o.reshape(t_new, HEADS, WIDTH), axis=1).astype(BF16)


def _fox_sample(q, kb, vb, ck, cache_k, cache_v, cache_lf_t, page_table):
    db, t_new, _ = q.shape
    n_pages = page_table.shape[1]
    n_pool, _, page = cache_k.shape
    pps = 16
    assert n_pages % pps == 0 and pps % 2 == 0 and page == LANES and t_new <= HEADS
    n_rows = t_new * HEADS
    pt = page_table.reshape(-1)

    def page_idx(n):
        return lambda b, s, pt_ref: (pt_ref[b * n_pages + n_pages - 1 - (s * pps + n)], 0, 0)

    new = lambda w: pl.BlockSpec((1, t_new, w), lambda b, s, pt_ref: (b, 0, 0))
    in_specs = ([new(WIDTH), new(WIDTH), new(WIDTH), new(LANES)]
                + [pl.BlockSpec((1, WIDTH, page), page_idx(n)) for n in range(pps)]
                + [pl.BlockSpec((1, WIDTH, page), page_idx(n)) for n in range(pps)]
                + [pl.BlockSpec((1, HEADS, page), page_idx(n)) for n in range(pps)])
    grid_spec = pltpu.PrefetchScalarGridSpec(
        num_scalar_prefetch=1,
        grid=(db, n_pages // pps),
        in_specs=in_specs,
        out_specs=new(WIDTH),
        scratch_shapes=[pltpu.VMEM((n_rows, WIDTH), BF16),
                        pltpu.VMEM((n_rows, 1), F32), pltpu.VMEM((n_rows, 1), F32),
                        pltpu.VMEM((n_rows, WIDTH), F32), pltpu.VMEM((HEADS, 1), F32),
                        pltpu.VMEM((3, page, WIDTH), F32)],
    )
    return pl.pallas_call(
        functools.partial(_fox_sample_kernel, t_new=t_new, pps=pps, page=page),
        out_shape=jax.ShapeDtypeStruct((db, t_new, WIDTH), BF16),
        grid_spec=grid_spec,
        compiler_params=_params(("arbitrary", "arbitrary")),
        name="fox_sample",
    )(pt, q, kb, vb, ck, *([cache_k] * pps), *([cache_v] * pps), *([cache_lf_t] * pps))


def _mlp_kernel(x_ref, or_ref, of_ref, gt_ref, lg_ref, lb_ref, wb_ref, wo_ref, g1_ref, b1_ref,
                wu_ref, wd_ref, g2_ref, b2_ref, y_ref, *, alpha, ff_chunk):
    d = x_ref.shape[1]
    hp = _layer_norm(x_ref[...], lg_ref[...], lb_ref[...])
    gt = gt_ref[...]
    merged = (gt[:, :d].astype(F32) * _dot(or_ref[...], wb_ref[0:WIDTH, :])
              + gt[:, d:].astype(F32) * _dot(of_ref[...], wb_ref[WIDTH:2 * WIDTH, :]))
    x1 = _layer_norm(alpha * hp + _dot(merged.astype(BF16), wo_ref[...]), g1_ref[...], b1_ref[...])
    x1b = x1.astype(BF16)
    h = jnp.zeros_like(x1)
    for c in range(wu_ref.shape[1] // ff_chunk):
        cs = slice(c * ff_chunk, (c + 1) * ff_chunk)
        up = jnp.maximum(_dot(x1b, wu_ref[:, cs]), 0.0)
        h = h + _dot((up * up).astype(BF16), wd_ref[cs, :])
    y_ref[...] = _layer_norm(alpha * x1 + h, g2_ref[...], b2_ref[...])


def _mlp(x2d, o_r, o_f, gates, ln_g, ln_b, wb, wo, g1, b1, wu, wd, g2, b2, alpha):
    m, d = x2d.shape
    tm = min(512, m)
    assert m % tm == 0
    row = lambda w: pl.BlockSpec((tm, w), lambda i: (i, 0))
    consts = (ln_g, ln_b, wb, wo, g1, b1, wu, wd, g2, b2)
    return pl.pallas_call(
        functools.partial(_mlp_kernel, alpha=alpha, ff_chunk=1024),
        out_shape=jax.ShapeDtypeStruct((m, d), F32),
        grid=(m // tm,),
        in_specs=[row(d), row(WIDTH), row(WIDTH), row(2 * d)] + [_const_spec(c.shape) for c in consts],
        out_specs=row(d),
        compiler_params=_params(("arbitrary",)),
        name="mlp",
    )(x2d, o_r, o_f, gates, *consts)


def kernel(x_prompt, x_sample, state_wkv, state_shift, cache_k, cache_v, cache_logf, page_table, ln_in_g, ln_in_b, w_in, rwkv_mu, rwkv_w0, rwkv_w2, rwkv_a0, rwkv_a2, rwkv_g2, rwkv_k_k, rwkv_k_a, rwkv_r_k, rwkv_gn_g, rwkv_gn_b, fox_b_f, w_branch, w_out, ln1_g, ln1_b, w_up, w_down, ln2_g, ln2_b):
    depth = w_in.shape[0]
    assert depth == 1, "the entry LayerNorm is fused into the first layer's kernels"
    bp, seq, d = x_prompt.shape
    db, dseq, _ = x_sample.shape
    n_pool, page = cache_k.shape[1], cache_k.shape[2]
    alpha = (2.0 * depth) ** 0.25
    row2 = lambda t: t.reshape(1, -1).astype(F32)
    ln_g, ln_b = row2(ln_in_g), row2(ln_in_b)

    l = 0
    w = w_in[l]
    fox_off = RWKV_PROJ
    wr = w[:, :RWKV_PROJ].astype(BF16)
    wq = w[:, fox_off:fox_off + WIDTH].astype(BF16)
    wt = w.T
    wkt = wt[fox_off + WIDTH:fox_off + 2 * WIDTH].astype(BF16)
    wvt = wt[fox_off + 2 * WIDTH:fox_off + 3 * WIDTH].astype(BF16)
    wft = jnp.pad(wt[fox_off + 3 * WIDTH:fox_off + 3 * WIDTH + HEADS], ((0, LANES - HEADS), (0, 0))).astype(BF16)
    wg = w[:, fox_off + 3 * WIDTH + HEADS:].astype(BF16)
    bf_col = jnp.pad(fox_b_f[l].reshape(HEADS, 1), ((0, LANES - HEADS), (0, 0))).astype(F32)
    proj_consts = (ln_g, ln_b, wq, wkt, wvt, wft, bf_col, wg)
    zl = jnp.zeros((DECAY_LORA, WIDTH), BF16)
    w2p = jnp.concatenate([rwkv_w2[l].astype(BF16), zl], axis=0)
    a2p = jnp.concatenate([zl, rwkv_a2[l].astype(BF16)], axis=0)
    rwkv_consts = (ln_g, ln_b, wr, row2(rwkv_mu[l]), row2(rwkv_w0[l]), w2p, row2(rwkv_a0[l]), a2p,
                   rwkv_g2[l].astype(BF16), row2(rwkv_k_k[l]), row2(rwkv_k_a[l]), row2(rwkv_r_k[l]),
                   row2(rwkv_gn_g[l]), row2(rwkv_gn_b[l]))
    mlp_consts = (ln_g, ln_b, w_branch[l].astype(BF16), w_out[l].astype(BF16), row2(ln1_g[l]),
                  row2(ln1_b[l]), w_up[l].astype(BF16), w_down[l].astype(BF16), row2(ln2_g[l]),
                  row2(ln2_b[l]))

    xp2 = x_prompt.reshape(bp * seq, d)
    q, kt32, vt32, ktb, vtb, lft, cq, ckt, gates = _proj(xp2, *proj_consts, seq, LOG2_E)
    to_seq = lambda t: t.reshape(bp, seq, t.shape[-1])
    o_f = _fox_prompt(to_seq(q), to_seq(cq), ktb, vtb, ckt)
    o_r, p_shift, p_wkv = _rwkv(x_prompt, seq, *rwkv_consts,
                                jnp.zeros((bp, RWKV_PROJ), F32),
                                jnp.zeros((bp, HEADS, HEAD_DIM, HEAD_DIM), F32))
    y_prompt = _mlp(xp2, o_r.reshape(bp * seq, WIDTH), o_f.reshape(bp * seq, WIDTH), gates,
                    *mlp_consts, alpha).reshape(bp, seq, d)
    heads_last = lambda t: jnp.transpose(t.reshape(bp, HEADS, HEAD_DIM, seq), (0, 3, 1, 2))[None]
    p_k = heads_last(kt32)
    p_v = heads_last(vt32)
    p_lf = jnp.swapaxes(lft, 1, 2)[None]

    xs2 = x_sample.reshape(db * dseq, d)
    q, kt32, vt32, _, _, lft, cq, ckt, gates = _proj(xs2, *proj_consts, dseq, 1.0)
    rows = lambda t: jnp.swapaxes(t, 1, 2).reshape(db * dseq, t.shape[1])
    k32, v32, lf, ck = rows(kt32), rows(vt32), rows(lft), rows(ckt)
    to_seq = lambda t: t.reshape(db, dseq, t.shape[-1])
    o_f = _fox_sample(to_seq(q.astype(F32)), to_seq(k32), to_seq(v32), to_seq(ck.astype(F32)),
                      jnp.transpose(cache_k[l], (0, 2, 3, 1)).reshape(n_pool, WIDTH, page),
                      jnp.transpose(cache_v[l], (0, 2, 3, 1)).reshape(n_pool, WIDTH, page),
                      jnp.swapaxes(cache_logf[l], 1, 2), page_table)
    xs_pad = jnp.pad(x_sample, ((0, 0), (0, CHUNK - dseq), (0, 0)))
    o_r, s_shift, s_wkv = _rwkv(xs_pad, dseq, *rwkv_consts, state_shift[l], state_wkv[l])
    y_sample = _mlp(xs2, o_r[:, :dseq].reshape(db * dseq, WIDTH), o_f.reshape(db * dseq, WIDTH), gates,
                    *mlp_consts, alpha).reshape(db, dseq, d)
    s_k = k32.reshape(1, db, dseq, HEADS, HEAD_DIM)
    s_v = v32.reshape(1, db, dseq, HEADS, HEAD_DIM)
    s_lf = lf.reshape(1, db, dseq, HEADS)

    return (y_prompt, y_sample, p_wkv[None], p_shift[None], p_k, p_v, p_lf,
            s_wkv[None], s_shift[None], s_k, s_v, s_lf)
```

```python
import functools
import math

import numpy as np
import jax
import jax.numpy as jnp
from jax import lax
from jax.experimental import pallas as pl
from jax.experimental.pallas import tpu as pltpu

F32 = jnp.float32
BF16 = jnp.bfloat16

HEADS = 8
HEAD_DIM = 64
WIDTH = HEADS * HEAD_DIM
DECAY_LORA = 64
AAA_LORA = 64
GATE_LORA = 128
RWKV_PROJ = 3 * WIDTH + DECAY_LORA + AAA_LORA + GATE_LORA
LORA_OFF = 3 * WIDTH
GATE_LORA_OFF = LORA_OFF + DECAY_LORA + AAA_LORA
GN_EPS = 64e-5
LN_EPS = 1e-5
QK_SCALE = HEAD_DIM ** -0.5
LOG2_E = math.log2(math.e)

LANES = 128
KV_BLOCK = 256
QUERY_BLOCK = 1024
CHUNK = 64
GROUP_HEADS = 4
GROUP_W = GROUP_HEADS * HEAD_DIM
N_GROUPS = HEADS // GROUP_HEADS
BIAS_PARTS = 3
BIAS_STRIDE = 8
VMEM_LIMIT = 56 * 1024 * 1024


def _dot(a, b):
    return jnp.dot(a, b, preferred_element_type=F32)


def _dot_nt(a, b):
    return lax.dot_general(a, b, (((1,), (1,)), ((), ())), preferred_element_type=F32)


def _dot_tn(a, b):
    return lax.dot_general(a, b, (((0,), (0,)), ((), ())), preferred_element_type=F32)


def _layer_norm(x, g, b):
    mu = jnp.mean(x, -1, keepdims=True)
    xc = x - mu
    var = jnp.mean(xc * xc, -1, keepdims=True)
    return xc * lax.rsqrt(var + LN_EPS) * g + b


def _split3(x):
    p1 = x.astype(BF16)
    r1 = x - p1.astype(F32)
    p2 = r1.astype(BF16)
    p3 = (r1 - p2.astype(F32)).astype(BF16)
    return p1, p2, p3


def _dot3(m_bf16, x):
    p1, p2, p3 = _split3(x)
    return _dot(m_bf16, p1) + _dot(m_bf16, p2) + _dot(m_bf16, p3)


def _dot3_rhs(x, m_bf16):
    p1, p2, p3 = _split3(x)
    return _dot(p1, m_bf16) + _dot(p2, m_bf16) + _dot(p3, m_bf16)


def _seq_tri(n, seq_len, upper=False):
    row = lax.broadcasted_iota(jnp.int32, (n, n), 0)
    col = lax.broadcasted_iota(jnp.int32, (n, n), 1)
    keep = (row <= col) if upper else (col <= row)
    if seq_len < n:
        sh = int(math.log2(seq_len))
        assert 1 << sh == seq_len
        keep = keep & (lax.shift_right_logical(row, sh) == lax.shift_right_logical(col, sh))
    return jnp.where(keep, 1.0, 0.0).astype(BF16)


def _round_robin(gens, stagger):
    steps = [0] * len(gens)
    live, started = [], 0
    while started < len(gens) or live:
        if started < len(gens) and (started == 0 or steps[started - 1] >= stagger or (started - 1) not in live):
            live.append(started)
            started += 1
        for i in list(live):
            try:
                next(gens[i])
                steps[i] += 1
            except StopIteration:
                live.remove(i)


def _const_spec(shape):
    nd = len(shape)
    return pl.BlockSpec(shape, lambda *_: (0,) * nd, pipeline_mode=pl.Buffered(1))


def _params(sem):
    return pltpu.CompilerParams(dimension_semantics=sem, vmem_limit_bytes=VMEM_LIMIT)


def _bias_consts():
    e = np.zeros((2 * BIAS_PARTS, LANES, LANES), np.float32)
    ones_q = np.zeros((1, LANES), np.float32)
    ones_k = np.zeros((1, LANES), np.float32)
    for h in range(HEADS):
        for j in range(BIAS_PARTS):
            e[j, h, BIAS_STRIDE * h + j] = 1.0
            e[BIAS_PARTS + j, h, BIAS_STRIDE * h + BIAS_PARTS + j] = 1.0
            ones_q[0, BIAS_STRIDE * h + BIAS_PARTS + j] = 1.0
            ones_k[0, BIAS_STRIDE * h + j] = 1.0
    return jnp.asarray(e, BF16), jnp.asarray(ones_q), jnp.asarray(ones_k)


def _proj_kernel(x_ref, g_ref, b_ref, wq_ref, wkt_ref, wvt_ref, wft_ref, bf_ref, wg_ref,
                 e_ref, et_ref, oq_ref, ok_ref,
                 q_ref, kt_ref, vt_ref, ktb_ref, vtb_ref, lft_ref, cq_ref, ckt_ref, gt_ref,
                 carry_ref, *, seq_len, tm, logit_scale):
    xn = _layer_norm(x_ref[...], g_ref[...], b_ref[...]).astype(BF16)

    def transposed(w_ref, o32_ref, ob_ref):
        t = _dot_nt(w_ref[...], xn)
        o32_ref[0] = t
        for jb in range(tm // KV_BLOCK):
            ob_ref[jb] = t[:, jb * KV_BLOCK:(jb + 1) * KV_BLOCK].astype(BF16)

    rowi = lax.broadcasted_iota(jnp.int32, (LANES, tm), 0)
    lft = jax.nn.log_sigmoid(_dot_nt(wft_ref[...], xn) + bf_ref[...])
    lft = jnp.where(rowi < HEADS, lft, 0.0)
    lft_ref[0] = lft[:HEADS, :]
    q_ref[...] = (_dot(xn, wq_ref[...]) * (QK_SCALE * logit_scale)).astype(BF16)

    ct = _dot3_rhs(lft, _seq_tri(tm, seq_len, upper=True))
    if seq_len > tm:
        @pl.when(pl.program_id(0) % (seq_len // tm) == 0)
        def _():
            carry_ref[...] = jnp.zeros_like(carry_ref)
        ct = ct + carry_ref[...]
        carry_ref[...] = ct[:, tm - 1:tm]
    transposed(wkt_ref, kt_ref, ktb_ref)

    if logit_scale != 1.0:
        ct = ct * logit_scale
    c1t, c2t, c3t = _split3(ct)
    ckt = ok_ref[...] - (_dot(et_ref[0], c1t) + _dot(et_ref[1], c2t) + _dot(et_ref[2], c3t))
    for jb in range(tm // KV_BLOCK):
        ckt_ref[jb] = ckt[:, jb * KV_BLOCK:(jb + 1) * KV_BLOCK].astype(BF16)
    transposed(wvt_ref, vt_ref, vtb_ref)
    c1, c2, c3 = _split3(ct.T)
    cq = _dot(c1, e_ref[0]) + _dot(c2, e_ref[1]) + _dot(c3, e_ref[2]) + oq_ref[...]
    cq_ref[...] = cq.astype(BF16)
    gt_ref[...] = jax.nn.sigmoid(_dot(xn, wg_ref[...])).astype(BF16)


def _proj(x2d, ln_g, ln_b, wq, wkt, wvt, wft, bf_col, wg, seq_len, logit_scale):
    m, d = x2d.shape
    tm = min(512, m)
    assert m % tm == 0 and (seq_len % tm == 0 or tm % seq_len == 0) and tm % KV_BLOCK == 0
    e, ones_q, ones_k = _bias_consts()
    e_q = e[:BIAS_PARTS]
    e_kt = jnp.swapaxes(e[BIAS_PARTS:], 1, 2)
    ones_k_col = ones_k.reshape(LANES, 1)
    if seq_len >= tm:
        tps = seq_len // tm
        groups, cols = m // seq_len, seq_len
        tr = lambda r: pl.BlockSpec((1, r, tm), lambda i: (i // tps, 0, i % tps))
    else:
        groups, cols = m // tm, tm
        tr = lambda r: pl.BlockSpec((1, r, tm), lambda i: (i, 0, 0))
    nkb = tm // KV_BLOCK
    trb = lambda r: pl.BlockSpec((nkb, r, KV_BLOCK), lambda i: (i, 0, 0))
    row = lambda w: pl.BlockSpec((tm, w), lambda i: (i, 0))
    out_shape = (
        jax.ShapeDtypeStruct((m, WIDTH), BF16),
        jax.ShapeDtypeStruct((groups, WIDTH, cols), F32),
        jax.ShapeDtypeStruct((groups, WIDTH, cols), F32),
        jax.ShapeDtypeStruct((m // KV_BLOCK, WIDTH, KV_BLOCK), BF16),
        jax.ShapeDtypeStruct((m // KV_BLOCK, WIDTH, KV_BLOCK), BF16),
        jax.ShapeDtypeStruct((groups, HEADS, cols), F32),
        jax.ShapeDtypeStruct((m, LANES), BF16),
        jax.ShapeDtypeStruct((m // KV_BLOCK, LANES, KV_BLOCK), BF16),
        jax.ShapeDtypeStruct((m, 2 * d), BF16),
    )
    consts = (ln_g, ln_b, wq, wkt, wvt, wft, bf_col, wg, e_q, e_kt, ones_q, ones_k_col)
    return pl.pallas_call(
        functools.partial(_proj_kernel, seq_len=seq_len, tm=tm, logit_scale=logit_scale),
        out_shape=out_shape,
        grid=(m // tm,),
        in_specs=[row(d)] + [_const_spec(c.shape) for c in consts],
        out_specs=(row(WIDTH), tr(WIDTH), tr(WIDTH), trb(WIDTH), trb(WIDTH), tr(HEADS),
                   row(LANES), trb(LANES), row(2 * d)),
        scratch_shapes=[pltpu.VMEM((LANES, 1), F32)],
        compiler_params=_params(("arbitrary",)),
        name="proj",
    )(x2d, *consts)


def _group_mask(rows):
    r = lax.broadcasted_iota(jnp.int32, (rows, GROUP_W), 0)
    c = lax.broadcasted_iota(jnp.int32, (rows, GROUP_W), 1)
    sh = int(math.log2(HEAD_DIM))
    return (lax.shift_right_logical(r, sh) & (GROUP_HEADS - 1)) == lax.shift_right_logical(c, sh)


def _bd(y, mask):
    yb = y.astype(BF16)
    return jnp.where(mask, jnp.concatenate([yb] * GROUP_HEADS, axis=0), jnp.zeros((), BF16))


def _state_to_bd(st, mask_bd):
    out = []
    for g in range(N_GROUPS):
        blk = st[g * GROUP_W:(g + 1) * GROUP_W, :]
        out.append(jnp.where(mask_bd, jnp.concatenate([blk] * GROUP_HEADS, axis=1), 0.0))
    return out


def _state_from_bd(state, mask_bd):
    rows = []
    for sg in state:
        sm = jnp.where(mask_bd, sg, 0.0)
        acc = sm[:, 0:HEAD_DIM]
        for h in range(1, GROUP_HEADS):
            acc = acc + sm[:, h * HEAD_DIM:(h + 1) * HEAD_DIM]
        rows.append(acc)
    return jnp.concatenate(rows, axis=0)


def _rwkv_kernel(x_ref, g_ref, b_ref, wr_ref, mu_ref, w0_ref, w2_ref, a0_ref, a2_ref, g2_ref,
                 kk_ref, ka_ref, rk_ref, gng_ref, gnb_ref, ones_ref, shift_ref, state_ref,
                 o_ref, shift_out_ref, state_out_ref,
                 s_ref, prev_ref, *, tm, t_real, chained):
    n_sub = 2 if tm // CHUNK >= 4 else 1
    rows = tm // n_sub
    nc = rows // CHUNK
    mask_bd = _group_mask(GROUP_W)
    rowi = lax.broadcasted_iota(jnp.int32, (rows, 1), 0)
    tri3 = jnp.concatenate([_seq_tri(CHUNK, CHUNK)] * BIAS_PARTS, axis=1)
    last_rows = {}
    state_written = set()

    if chained:
        @pl.when(pl.program_id(1) == 0)
        def _():
            prev_ref[...] = shift_ref[0]
            for g, sg in enumerate(_state_to_bd(state_ref[0], mask_bd)):
                s_ref[g] = sg

    def prologue(si):
        r0, c0 = si * rows, si * nc
        x = x_ref[...].reshape(tm, x_ref.shape[-1])[r0:r0 + rows]
        xn = _layer_norm(x, g_ref[...], b_ref[...]).astype(BF16)
        p = _dot(xn, wr_ref[...])
        last_rows[si] = p[rows - 1:rows, :]
        yield
        if chained:
            prev = prev_ref[...] if si == 0 else last_rows[si - 1]
            p_prev = jnp.where(rowi == 0, prev, pltpu.roll(p, 1, 0))
            if si == n_sub - 1:
                prev_ref[...] = p[rows - 1:rows, :]
                shift_out_ref[0] = p[rows - 1:rows, :]
        else:
            first = jnp.concatenate([jnp.broadcast_to(shift_ref[c0 + c], (CHUNK, RWKV_PROJ)) for c in range(nc)],
                                    axis=0)
            p_prev = jnp.where((rowi & (CHUNK - 1)) == 0, first, pltpu.roll(p, 1, 0))
            for c in range(nc):
                shift_out_ref[c0 + c] = p[c * CHUNK + t_real - 1:c * CHUNK + t_real, :]
        pm = p + (p_prev - p) * mu_ref[...]
        r = pm[:, 0:WIDTH]
        k = pm[:, WIDTH:2 * WIDTH]
        v = pm[:, 2 * WIDTH:3 * WIDTH]
        dwa = pm[:, LORA_OFF:GATE_LORA_OFF]
        dg = pm[:, GATE_LORA_OFF:RWKV_PROJ]
        w_log = -jax.nn.softplus(-(w0_ref[...] + _dot(jnp.tanh(dwa).astype(BF16), w2_ref[...]))) - 0.5
        lw = -jnp.exp(w_log)
        a = jax.nn.sigmoid(a0_ref[...] + _dot(dwa.astype(BF16), a2_ref[...]))
        gate = _dot(jax.nn.sigmoid(dg).astype(BF16), g2_ref[...])
        yield
        kk = k * kk_ref[...]
        ss = _dot((kk * kk).astype(BF16), ones_ref[...])
        kk = kk * lax.rsqrt(jnp.maximum(ss, 1e-24))
        k = k * (1.0 + (a - 1.0) * ka_ref[...])
        bonus = _dot((r * k * rk_ref[...]).astype(BF16), ones_ref[...]) * v
        kka = kk * a
        if not chained and t_real < CHUNK:
            live = (rowi & (CHUNK - 1)) < t_real
            lw = jnp.where(live, lw, 0.0)
            r, k, v, kk, kka = (jnp.where(live, t, 0.0) for t in (r, k, v, kk, kka))
        yield
        lw_parts = _split3(lw)
        cl = jnp.concatenate(
            [_dot(tri3, jnp.concatenate([t[c * CHUNK:(c + 1) * CHUNK, :] for t in lw_parts], axis=0))
             for c in range(nc)], axis=0) if nc > 1 else _dot(tri3, jnp.concatenate(lw_parts, axis=0))
        e_ng = jnp.exp(-cl)
        return dict(rt=r * jnp.exp(cl), at=-kk * jnp.exp(cl - lw), bt=kka * e_ng, kt=k * e_ng,
                    v=v, k=k, kka=kka, cl=cl, bonus=bonus, gate=gate)

    row = lax.broadcasted_iota(jnp.int32, (CHUNK, GROUP_W), 0)
    col = lax.broadcasted_iota(jnp.int32, (CHUNK, GROUP_W), 1) & (CHUNK - 1)
    strict = col < row
    incl = col <= row
    blk16 = lax.shift_right_logical(row, 4) == lax.shift_right_logical(col, 4)
    blk32 = lax.shift_right_logical(row, 5) == lax.shift_right_logical(col, 5)

    def mm(x, y):
        return _dot(x.astype(BF16), _bd(y, mask_bd))

    probs = [(c, g) for c in range(nc) for g in range(N_GROUPS)]
    sl = lambda x, c, g: x[c * CHUNK:(c + 1) * CHUNK, g * GROUP_W:(g + 1) * GROUP_W]
    each = lambda f, *ls: [f(*xs) for xs in zip(*ls)]
    add = lambda *xs: functools.reduce(lambda a_, b_: a_ + b_, xs)
    stack = lambda a_, b_: jnp.concatenate([a_, b_], axis=0)
    top = lambda xs: [x[:CHUNK] for x in xs]
    bot = lambda xs: [x[CHUNK:] for x in xs]

    def scan(si, d):
        rt, at, bt, kt, v, k, kka, cl = (d[n] for n in ("rt", "at", "bt", "kt", "v", "k", "kka", "cl"))
        rt_, at_, bt_, kt_, v_ = ([sl(x, c, g) for c, g in probs] for x in (rt, at, bt, kt, v))
        cl_end = [cl[(c + 1) * CHUNK - 1:(c + 1) * CHUNK, :] for c in range(nc)]
        g_end = [jnp.exp(ce) for ce in cl_end]
        e_end = [jnp.exp(cl_end[c] - cl[c * CHUNK:(c + 1) * CHUNK, :]) for c in range(nc)]
        bh_ = [sl(kka, c, g) * e_end[c][:, g * GROUP_W:(g + 1) * GROUP_W] for c, g in probs]
        kh_ = [sl(k, c, g) * e_end[c][:, g * GROUP_W:(g + 1) * GROUP_W] for c, g in probs]
        yield
        ar_ = each(lambda a_, r_: jnp.concatenate([a_, r_], axis=0).astype(BF16), at_, rt_)
        gb_ = each(lambda ar, b_: _dot_nt(ar, _bd(b_, mask_bd)), ar_, bt_)
        yield
        gk_ = each(lambda ar, k_: _dot_nt(ar, _bd(k_, mask_bd)), ar_, kt_)
        n_ab = [jnp.where(strict, x[:CHUNK], 0.0) for x in gb_]
        a_ak = [jnp.where(strict, x[:CHUNK], 0.0) for x in gk_]
        p_rb = [jnp.where(incl, x[CHUNK:], 0.0) for x in gb_]
        p_rk = [jnp.where(incl, x[CHUNK:], 0.0) for x in gk_]
        yield
        uvp = each(mm, each(stack, a_ak, p_rk), v_)
        uv_, prkv = top(uvp), bot(uvp)
        yield
        n_d = [jnp.where(blk16, x, 0.0) for x in n_ab]
        n_1 = [jnp.where(blk32 & jnp.logical_not(blk16), x, 0.0) for x in n_ab]
        n_2 = [jnp.where(blk32, 0.0, x) for x in n_ab]
        n2 = each(mm, n_d, n_d)
        yield
        n34 = each(mm, each(stack, n_d, n2), n2)
        n3, n4 = top(n34), bot(n34)
        q4 = each(add, n_d, n2, n3)
        yield
        m48 = each(mm, each(stack, q4, n4), n4)
        q8 = each(add, q4, n4, top(m48))
        n8 = bot(m48)
        yield
        t16 = each(add, q8, n8, each(mm, q8, n8))
        yield
        x1 = each(add, n_1, each(mm, n_1, t16))
        yield
        t32 = each(add, t16, x1, each(mm, t16, x1))
        yield
        x2 = each(add, n_2, each(mm, n_2, t32))
        yield
        t_ = each(add, t32, x2, each(mm, t32, x2))
        yield
        wt_ = each(add, at_, each(mm, t_, at_))
        yield
        ut_ = each(add, uv_, each(mm, t_, uv_))
        yield
        mw_ = each(lambda w_, b_: jnp.where(mask_bd, _dot_tn(w_.astype(BF16), b_.astype(BF16)), 0.0).astype(BF16),
                   wt_, bh_)
        yield
        dd_ = each(lambda u_, vv, b_, k_: jnp.where(mask_bd, _dot_tn(
            jnp.concatenate([u_, vv], axis=0).astype(BF16), jnp.concatenate([b_, k_], axis=0).astype(BF16)), 0.0),
                   ut_, v_, bh_, kh_)
        yield
        qq_ = each(lambda r_, x: (r_ + x).astype(BF16), rt_, each(mm, p_rb, wt_))
        yield
        o0_ = each(add, each(mm, p_rb, ut_), prkv)
        yield

        if chained:
            while si > 0 and (si - 1) not in state_written:
                yield
            state = [s_ref[g] for g in range(N_GROUPS)]
        o_chunks = []
        for c in range(nc):
            if not chained:
                state = _state_to_bd(state_ref[si * nc + c], mask_bd)
            o_groups = []
            for g in range(N_GROUPS):
                j = c * N_GROUPS + g
                s0 = state[g]
                s0b = s0.astype(BF16)
                o_groups.append(_dot_nt(qq_[j], s0b) + o0_[j])
                state[g] = s0 * g_end[c][:, g * GROUP_W:(g + 1) * GROUP_W] + _dot(s0b, mw_[j]) + dd_[j]
            o_chunks.append(jnp.concatenate(o_groups, axis=1))
            if not chained:
                state_out_ref[si * nc + c] = _state_from_bd(state, mask_bd)
            yield
        if chained:
            for g in range(N_GROUPS):
                s_ref[g] = state[g]
            state_written.add(si)
        o = jnp.concatenate(o_chunks, axis=0) if len(o_chunks) > 1 else o_chunks[0]

        inv_n = 1.0 / HEAD_DIM
        mean = _dot(o.astype(BF16), ones_ref[...]) * inv_n
        oc = o - mean
        var = _dot((oc * oc).astype(BF16), ones_ref[...]) * inv_n
        on = oc * lax.rsqrt(var + GN_EPS) * gng_ref[...] + gnb_ref[...]
        out = ((on + d["bonus"]) * d["gate"]).astype(BF16)
        if chained:
            o_ref[0, si * rows:(si + 1) * rows, :] = out
        else:
            o_ref[si * nc:(si + 1) * nc] = out.reshape(nc, CHUNK, WIDTH)

    def pipeline(si):
        d = yield from prologue(si)
        yield
        yield from scan(si, d)

    _round_robin([pipeline(si) for si in range(n_sub)], stagger=4)

    if chained:
        @pl.when(pl.program_id(1) == pl.num_programs(1) - 1)
        def _():
            state_out_ref[0] = _state_from_bd([s_ref[g] for g in range(N_GROUPS)], mask_bd)


def _rwkv(x3d, t_real, ln_g, ln_b, wr, mu, w0, w2p, a0, a2p, g2, k_k, k_a, r_k, gn_g, gn_b,
          shift_in, state_in):
    nseq, t, d = x3d.shape
    chained = t_real == t
    if chained:
        tm, nb = min(512, t), 1
        assert t % tm == 0 and tm % CHUNK == 0
        grid = (nseq, t // tm)
        x_spec = pl.BlockSpec((1, tm, d), lambda b, i: (b, i, 0))
        o_spec = pl.BlockSpec((1, tm, WIDTH), lambda b, i: (b, i, 0))
    else:
        nb = min(8, nseq)
        tm = nb * CHUNK
        assert t == CHUNK and nseq % nb == 0
        grid = (nseq // nb, 1)
        x_spec = pl.BlockSpec((nb, CHUNK, d), lambda b, i: (b, 0, 0))
        o_spec = pl.BlockSpec((nb, CHUNK, WIDTH), lambda b, i: (b, 0, 0))
    ones_bd = jnp.asarray(np.kron(np.eye(HEADS, dtype=np.float32),
                                  np.ones((HEAD_DIM, HEAD_DIM), np.float32)), BF16)
    consts = (ln_g, ln_b, wr, mu, w0, w2p, a0, a2p, g2, k_k, k_a, r_k, gn_g, gn_b, ones_bd)
    state2d = state_in.reshape(nseq, HEADS * HEAD_DIM, HEAD_DIM)
    shift3d = shift_in.reshape(nseq, 1, RWKV_PROJ)
    seq_spec = lambda shape: pl.BlockSpec((nb,) + shape, lambda b, i: (b, 0, 0))
    o, shift_out, state_out = pl.pallas_call(
        functools.partial(_rwkv_kernel, tm=tm, t_real=t_real, chained=chained),
        out_shape=(jax.ShapeDtypeStruct((nseq, t, WIDTH), BF16),
                   jax.ShapeDtypeStruct((nseq, 1, RWKV_PROJ), F32),
                   jax.ShapeDtypeStruct((nseq, HEADS * HEAD_DIM, HEAD_DIM), F32)),
        grid=grid,
        in_specs=[x_spec] + [_const_spec(c.shape) for c in consts]
                 + [seq_spec((1, RWKV_PROJ)), seq_spec((HEADS * HEAD_DIM, HEAD_DIM))],
        out_specs=(o_spec, seq_spec((1, RWKV_PROJ)), seq_spec((HEADS * HEAD_DIM, HEAD_DIM))),
        scratch_shapes=[pltpu.VMEM((N_GROUPS, GROUP_W, GROUP_W), F32),
                        pltpu.VMEM((1, RWKV_PROJ), F32)],
        compiler_params=_params(("arbitrary", "arbitrary")),
        name="rwkv",
    )(x3d, *consts, shift3d, state2d)
    return (o, shift_out.reshape(nseq, RWKV_PROJ),
            state_out.reshape(nseq, HEADS, HEAD_DIM, HEAD_DIM))


def _softmax_step(carry, s, v_blk):
    m, l, acc = carry
    m_new = jnp.maximum(m, jnp.max(s, axis=-1, keepdims=True))
    alpha = jnp.exp(m - m_new)
    p = jnp.exp(s - m_new)
    l = alpha * l + jnp.sum(p, axis=-1, keepdims=True)
    acc = alpha * acc + _dot(p.astype(BF16), v_blk)
    return m_new, l, acc


def _fox_prompt_kernel(q_ref, cq_ref, k_ref, v_ref, ck_ref, o_ref, qa_ref, m_ref, acc_ref, *, tq):
    i = pl.program_id(1)
    lane = lax.broadcasted_iota(jnp.int32, (tq, LANES), 1)
    cq = cq_ref[0]
    zero = jnp.zeros((), BF16)
    for h in range(HEADS):
        q_pair = q_ref[0, :, (h // 2) * LANES:(h // 2 + 1) * LANES]
        q_h = jnp.where(lax.shift_right_logical(lane, 6) == (h % 2), q_pair, zero)
        c_h = jnp.where(lax.shift_right_logical(lane, 3) == h, cq, zero)
        qa_ref[h] = jnp.concatenate([q_h, c_h], axis=1)
    m_ref[...] = jnp.full(m_ref.shape, -jnp.inf, F32)
    acc_ref[...] = jnp.zeros(acc_ref.shape, F32)
    tk = KV_BLOCK
    ones = jnp.ones((LANES, tk), BF16)

    def block(j, r0, mask):
        ck = ck_ref[j]
        for pair in range(HEADS // 2):
            ls = slice(pair * LANES, (pair + 1) * LANES)
            ka = jnp.concatenate([k_ref[j, ls, :], ck], axis=0)
            v_aug = jnp.concatenate([v_ref[j, ls, :], ones], axis=0)
            for h in (2 * pair, 2 * pair + 1):
                s = _dot(qa_ref[h, r0:, :], ka)
                if mask is not None:
                    s = jnp.where(mask, s, -jnp.inf)
                parts = [s[:, t * LANES:(t + 1) * LANES] for t in range(tk // LANES)]
                m_old = m_ref[h, r0:, :]
                m_new = jnp.maximum(m_old, jnp.max(functools.reduce(jnp.maximum, parts), axis=-1, keepdims=True))
                alpha = jnp.exp2(m_old - m_new)
                p = jnp.concatenate([jnp.exp2(t - m_new) for t in parts], axis=1).astype(BF16)
                m_ref[h, r0:, :] = m_new
                acc_ref[h, r0:, :] = (jnp.concatenate([alpha, alpha], axis=1) * acc_ref[h, r0:, :]
                                      + _dot_nt(p, v_aug))

    def body(j, carry):
        block(j, 0, None)
        return carry

    n_full = i * (tq // tk)
    lax.fori_loop(0, n_full, body, 0)
    for d in range(tq // tk):
        rows = tq - d * tk
        row = lax.broadcasted_iota(jnp.int32, (rows, tk), 0)
        col = lax.broadcasted_iota(jnp.int32, (rows, tk), 1)
        block(n_full + d, d * tk, col <= row)
    for pair in range(HEADS // 2):
        lo = acc_ref[2 * pair, :, 0:LANES] / acc_ref[2 * pair, :, LANES:2 * LANES]
        hi = acc_ref[2 * pair + 1, :, 0:LANES] / acc_ref[2 * pair + 1, :, LANES:2 * LANES]
        o_ref[0, :, pair * LANES:(pair + 1) * LANES] = jnp.where(
            lax.shift_right_logical(lane, 6) == 0, lo, hi).astype(BF16)


def _fox_prompt(q, cq, ktb, vtb, ckt):
    b, s, _ = q.shape
    tq = min(QUERY_BLOCK, s)
    assert s % tq == 0 and tq % KV_BLOCK == 0
    nkb = s // KV_BLOCK
    blk = lambda w: pl.BlockSpec((1, tq, w), lambda bi, i: (bi, i, 0))
    full = lambda r: pl.BlockSpec((nkb, r, KV_BLOCK), lambda bi, i: (bi, 0, 0))
    return pl.pallas_call(
        functools.partial(_fox_prompt_kernel, tq=tq),
        out_shape=jax.ShapeDtypeStruct((b, s, WIDTH), BF16),
        grid=(b, s // tq),
        in_specs=[blk(WIDTH), blk(LANES), full(WIDTH), full(WIDTH), full(LANES)],
        out_specs=blk(WIDTH),
        scratch_shapes=[pltpu.VMEM((HEADS, tq, 2 * LANES), BF16),
                        pltpu.VMEM((HEADS, tq, LANES), F32),
                        pltpu.VMEM((HEADS, tq, 2 * LANES), F32)],
        compiler_params=_params(("arbitrary", "arbitrary")),
        name="fox_prompt",
    )(q, cq, ktb, vtb, ckt)


def _fox_sample_kernel(pt_ref, q_ref, kn_ref, vn_ref, ckn_ref, *refs, t_new, pps, page):
    k_refs = refs[0:pps]
    v_refs = refs[pps:2 * pps]
    lf_refs = refs[2 * pps:3 * pps]
    o_ref = refs[3 * pps]
    qrow_ref, m_ref, l_ref, acc_ref, run_ref, pad_ref = refs[3 * pps + 1:]
    del pt_ref
    step = pl.program_id(1)
    n_rows = t_new * HEADS
    rowh = lax.broadcasted_iota(jnp.int32, (n_rows, WIDTH), 0) & (HEADS - 1)
    laneh = lax.shift_right_logical(lax.broadcasted_iota(jnp.int32, (n_rows, WIDTH), 1), 6)
    head_mask = rowh == laneh

    @pl.when(step == 0)
    def _():
        q = q_ref[0]
        rows = jnp.concatenate([jnp.broadcast_to(q[t:t + 1, :], (HEADS, WIDTH)) for t in range(t_new)],
                               axis=0)
        qrow_ref[...] = jnp.where(head_mask, rows, 0.0).astype(BF16)
        m_ref[...] = jnp.full(m_ref.shape, -jnp.inf, F32)
        l_ref[...] = jnp.zeros(l_ref.shape, F32)
        acc_ref[...] = jnp.zeros(acc_ref.shape, F32)
        run_ref[...] = jnp.zeros(run_ref.shape, F32)

    qrows = qrow_ref[...]
    lf = jnp.concatenate([r[0] for r in lf_refs], axis=0)
    newer = lax.broadcasted_iota(jnp.int32, (page, page), 0) > lax.broadcasted_iota(jnp.int32, (page, page), 1)
    suf = _dot3_rhs(lf, jnp.where(newer, 1.0, 0.0).astype(BF16))
    run = run_ref[...]
    biases = []
    for n in range(pps):
        blk = slice(n * HEADS, (n + 1) * HEADS)
        biases.append(jnp.concatenate([suf[blk] + run] * t_new, axis=0))
        run = run + (suf[blk][:, 0:1] + lf[blk][:, 0:1])
    run_ref[...] = run
    pair = lambda refs, n: jnp.concatenate([refs[n][0].astype(BF16), refs[n + 1][0].astype(BF16)], axis=1)
    scores = [_dot(qrows, pair(k_refs, n)) + jnp.concatenate(biases[n:n + 2], axis=1)
              for n in range(0, pps, 2)]
    m_old = m_ref[...]
    m_new = jnp.maximum(m_old, jnp.max(functools.reduce(jnp.maximum, scores), axis=-1, keepdims=True))
    alpha = jnp.exp(m_old - m_new)
    ps = [jnp.exp(s - m_new) for s in scores]
    pv = functools.reduce(lambda a, b: a + b,
                          [_dot_nt(p.astype(BF16), pair(v_refs, 2 * i)) for i, p in enumerate(ps)])
    m_ref[...] = m_new
    l_ref[...] = alpha * l_ref[...] + jnp.sum(functools.reduce(lambda a, b: a + b, ps), axis=-1, keepdims=True)
    acc_ref[...] = alpha * acc_ref[...] + pv

    @pl.when(step == pl.num_programs(1) - 1)
    def _():
        pad_ref[...] = jnp.zeros(pad_ref.shape, F32)
        pad_ref[0, 0:t_new, :] = kn_ref[0]
        pad_ref[1, 0:t_new, :] = vn_ref[0]
        pad_ref[2, 0:t_new, 0:LANES] = ckn_ref[0]
        kn = pad_ref[0].astype(BF16)
        vn = pad_ref[1].astype(BF16)
        ckn = pad_ref[2, :, 0:LANES].astype(BF16)
        r8 = lax.broadcasted_iota(jnp.int32, (n_rows, LANES), 0) & (HEADS - 1)
        l8 = lax.broadcasted_iota(jnp.int32, (n_rows, LANES), 1)
        sel = (lax.shift_right_logical(l8, 3) == r8) & ((l8 & (BIAS_STRIDE - 1)) >= BIAS_PARTS) \
            & ((l8 & (BIAS_STRIDE - 1)) < 2 * BIAS_PARTS)
        cqr = jnp.where(sel, 1.0, 0.0).astype(BF16)
        s = _dot_nt(jnp.concatenate([qrows, cqr], axis=1), jnp.concatenate([kn, ckn], axis=1))
        key = lax.broadcasted_iota(jnp.int32, (n_rows, page), 1)
        tok = lax.shift_right_logical(lax.broadcasted_iota(jnp.int32, (n_rows, page), 0), 3)
        s = jnp.where(key <= tok, s, -jnp.inf)
        _, l, acc = _softmax_step((m_ref[...], l_ref[...], acc_ref[...]), s, vn)
        o = jnp.where(head_mask, acc / l, 0.0)
        o_ref[0] = jnp.sum(o.reshape(t_new, HEADS, WIDTH), axis=1).astype(BF16)


def _fox_sample(q, kb, vb, ck, cache_k, cache_v, cache_lf_t, page_table):
    db, t_new, _ = q.shape
    n_pages = page_table.shape[1]
    n_pool, _, page = cache_k.shape
    pps = 16
    assert n_pages % pps == 0 and pps % 2 == 0 and page == LANES and t_new <= HEADS
    n_rows = t_new * HEADS
    pt = page_table.reshape(-1)

    def page_idx(n):
        return lambda b, s, pt_ref: (pt_ref[b * n_pages + n_pages - 1 - (s * pps + n)], 0, 0)

    new = lambda w: pl.BlockSpec((1, t_new, w), lambda b, s, pt_ref: (b, 0, 0))
    in_specs = ([new(WIDTH), new(WIDTH), new(WIDTH), new(LANES)]
                + [pl.BlockSpec((1, WIDTH, page), page_idx(n)) for n in range(pps)]
                + [pl.BlockSpec((1, WIDTH, page), page_idx(n)) for n in range(pps)]
                + [pl.BlockSpec((1, HEADS, page), page_idx(n)) for n in range(pps)])
    grid_spec = pltpu.PrefetchScalarGridSpec(
        num_scalar_prefetch=1,
        grid=(db, n_pages // pps),
        in_specs=in_specs,
        out_specs=new(WIDTH),
        scratch_shapes=[pltpu.VMEM((n_rows, WIDTH), BF16),
                        pltpu.VMEM((n_rows, 1), F32), pltpu.VMEM((n_rows, 1), F32),
                        pltpu.VMEM((n_rows, WIDTH), F32), pltpu.VMEM((HEADS, 1), F32),
                        pltpu.VMEM((3, page, WIDTH), F32)],
    )
    return pl.pallas_call(
        functools.partial(_fox_sample_kernel, t_new=t_new, pps=pps, page=page),
        out_shape=jax.ShapeDtypeStruct((db, t_new, WIDTH), BF16),
        grid_spec=grid_spec,
        compiler_params=_params(("arbitrary", "arbitrary")),
        name="fox_sample",
    )(pt, q, kb, vb, ck, *([cache_k] * pps), *([cache_v] * pps), *([cache_lf_t] * pps))


def _mlp_kernel(x_ref, or_ref, of_ref, gt_ref, lg_ref, lb_ref, wb_ref, wo_ref, g1_ref, b1_ref,
                wu_ref, wd_ref, g2_ref, b2_ref, y_ref, *, alpha, ff_chunk):
    tm, d = x_ref.shape
    n_sub = 2 if tm >= 256 else 1
    rows = tm // n_sub

    def half(si):
        rs = slice(si * rows, (si + 1) * rows)
        hp = _layer_norm(x_ref[rs, :], lg_ref[...], lb_ref[...])
        gt = gt_ref[rs, :]
        yield
        merged = (gt[:, :d].astype(F32) * _dot(or_ref[rs, :], wb_ref[0:WIDTH, :])
                  + gt[:, d:].astype(F32) * _dot(of_ref[rs, :], wb_ref[WIDTH:2 * WIDTH, :]))
        yield
        pre = alpha * hp + _dot(merged.astype(BF16), wo_ref[...])
        yield
        x1 = _layer_norm(pre, g1_ref[...], b1_ref[...])
        x1b = x1.astype(BF16)
        yield
        h = jnp.zeros_like(x1)
        for c in range(wu_ref.shape[1] // ff_chunk):
            cs = slice(c * ff_chunk, (c + 1) * ff_chunk)
            up = jnp.maximum(_dot(x1b, wu_ref[:, cs]), 0.0)
            yield
            h = h + _dot((up * up).astype(BF16), wd_ref[cs, :])
            yield
        y_ref[rs, :] = _layer_norm(alpha * x1 + h, g2_ref[...], b2_ref[...])

    _round_robin([half(si) for si in range(n_sub)], stagger=4)


def _mlp(x2d, o_r, o_f, gates, ln_g, ln_b, wb, wo, g1, b1, wu, wd, g2, b2, alpha):
    m, d = x2d.shape
    tm = min(512, m)
    assert m % tm == 0
    row = lambda w: pl.BlockSpec((tm, w), lambda i: (i, 0))
    consts = (ln_g, ln_b, wb, wo, g1, b1, wu, wd, g2, b2)
    return pl.pallas_call(
        functools.partial(_mlp_kernel, alpha=alpha, ff_chunk=1024),
        out_shape=jax.ShapeDtypeStruct((m, d), F32),
        grid=(m // tm,),
        in_specs=[row(d), row(WIDTH), row(WIDTH), row(2 * d)] + [_const_spec(c.shape) for c in consts],
        out_specs=row(d),
        compiler_params=_params(("arbitrary",)),
        name="mlp",
    )(x2d, o_r, o_f, gates, *consts)


def kernel(x_prompt, x_sample, state_wkv, state_shift, cache_k, cache_v, cache_logf, page_table, ln_in_g, ln_in_b, w_in, rwkv_mu, rwkv_w0, rwkv_w2, rwkv_a0, rwkv_a2, rwkv_g2, rwkv_k_k, rwkv_k_a, rwkv_r_k, rwkv_gn_g, rwkv_gn_b, fox_b_f, w_branch, w_out, ln1_g, ln1_b, w_up, w_down, ln2_g, ln2_b):
    depth = w_in.shape[0]
    assert depth == 1, "the entry LayerNorm is fused into the first layer's kernels"
    bp, seq, d = x_prompt.shape
    db, dseq, _ = x_sample.shape
    n_pool, page = cache_k.shape[1], cache_k.shape[2]
    alpha = (2.0 * depth) ** 0.25
    row2 = lambda t: t.reshape(1, -1).astype(F32)
    ln_g, ln_b = row2(ln_in_g), row2(ln_in_b)

    l = 0
    w = w_in[l]
    fox_off = RWKV_PROJ
    wr = w[:, :RWKV_PROJ].astype(BF16)
    wq = w[:, fox_off:fox_off + WIDTH].astype(BF16)
    wt = w.T
    wkt = wt[fox_off + WIDTH:fox_off + 2 * WIDTH].astype(BF16)
    wvt = wt[fox_off + 2 * WIDTH:fox_off + 3 * WIDTH].astype(BF16)
    wft = jnp.pad(wt[fox_off + 3 * WIDTH:fox_off + 3 * WIDTH + HEADS], ((0, LANES - HEADS), (0, 0))).astype(BF16)
    wg = w[:, fox_off + 3 * WIDTH + HEADS:].astype(BF16)
    bf_col = jnp.pad(fox_b_f[l].reshape(HEADS, 1), ((0, LANES - HEADS), (0, 0))).astype(F32)
    proj_consts = (ln_g, ln_b, wq, wkt, wvt, wft, bf_col, wg)
    zl = jnp.zeros((DECAY_LORA, WIDTH), BF16)
    w2p = jnp.concatenate([rwkv_w2[l].astype(BF16), zl], axis=0)
    a2p = jnp.concatenate([zl, rwkv_a2[l].astype(BF16)], axis=0)
    rwkv_consts = (ln_g, ln_b, wr, row2(rwkv_mu[l]), row2(rwkv_w0[l]), w2p, row2(rwkv_a0[l]), a2p,
                   rwkv_g2[l].astype(BF16), row2(rwkv_k_k[l]), row2(rwkv_k_a[l]), row2(rwkv_r_k[l]),
                   row2(rwkv_gn_g[l]), row2(rwkv_gn_b[l]))
    mlp_consts = (ln_g, ln_b, w_branch[l].astype(BF16), w_out[l].astype(BF16), row2(ln1_g[l]),
                  row2(ln1_b[l]), w_up[l].astype(BF16), w_down[l].astype(BF16), row2(ln2_g[l]),
                  row2(ln2_b[l]))

    xp2 = x_prompt.reshape(bp * seq, d)
    q, kt32, vt32, ktb, vtb, lft, cq, ckt, gates = _proj(xp2, *proj_consts, seq, LOG2_E)
    to_seq = lambda t: t.reshape(bp, seq, t.shape[-1])
    o_f = _fox_prompt(to_seq(q), to_seq(cq), ktb, vtb, ckt)
    o_r, p_shift, p_wkv = _rwkv(x_prompt, seq, *rwkv_consts,
                                jnp.zeros((bp, RWKV_PROJ), F32),
                                jnp.zeros((bp, HEADS, HEAD_DIM, HEAD_DIM), F32))
    y_prompt = _mlp(xp2, o_r.reshape(bp * seq, WIDTH), o_f.reshape(bp * seq, WIDTH), gates,
                    *mlp_consts, alpha).reshape(bp, seq, d)
    heads_last = lambda t: jnp.transpose(t.reshape(bp, HEADS, HEAD_DIM, seq), (0, 3, 1, 2))[None]
    p_k = heads_last(kt32)
    p_v = heads_last(vt32)
    p_lf = jnp.swapaxes(lft, 1, 2)[None]

    xs2 = x_sample.reshape(db * dseq, d)
    q, kt32, vt32, _, _, lft, cq, ckt, gates = _proj(xs2, *proj_consts, dseq, 1.0)
    rows = lambda t: jnp.swapaxes(t, 1, 2).reshape(db * dseq, t.shape[1])
    k32, v32, lf, ck = rows(kt32), rows(vt32), rows(lft), rows(ckt)
    to_seq = lambda t: t.reshape(db, dseq, t.shape[-1])
    o_f = _fox_sample(to_seq(q.astype(F32)), to_seq(k32), to_seq(v32), to_seq(ck.astype(F32)),
                      jnp.transpose(cache_k[l], (0, 2, 3, 1)).reshape(n_pool, WIDTH, page),
                      jnp.transpose(cache_v[l], (0, 2, 3, 1)).reshape(n_pool, WIDTH, page),
                      jnp.swapaxes(cache_logf[l], 1, 2), page_table)
    xs_pad = jnp.pad(x_sample, ((0, 0), (0, CHUNK - dseq), (0, 0)))
    o_r, s_shift, s_wkv = _rwkv(xs_pad, dseq, *rwkv_consts, state_shift[l], state_wkv[l])
    y_sample = _mlp(xs2, o_r[:, :dseq].reshape(db * dseq, WIDTH), o_f.reshape(db * dseq, WIDTH), gates,
                    *mlp_consts, alpha).reshape(db, dseq, d)
    s_k = k32.reshape(1, db, dseq, HEADS, HEAD_DIM)
    s_v = v32.reshape(1, db, dseq, HEADS, HEAD_DIM)
    s_lf = lf.reshape(1, db, dseq, HEADS)

    return (y_prompt, y_sample, p_wkv[None], p_shift[None], p_k, p_v, p_lf,
            s_wkv[None], s_shift[None], s_k, s_v, s_lf)
```

```python
import functools
import math

import numpy as np
import jax
import jax.numpy as jnp
from jax import lax
from jax.experimental import pallas as pl
from jax.experimental.pallas import tpu as pltpu

F32 = jnp.float32
BF16 = jnp.bfloat16

HEADS = 8
HEAD_DIM = 64
WIDTH = HEADS * HEAD_DIM
DECAY_LORA = 64
AAA_LORA = 64
GATE_LORA = 128
RWKV_PROJ = 3 * WIDTH + DECAY_LORA + AAA_LORA + GATE_LORA
LORA_OFF = 3 * WIDTH
GATE_LORA_OFF = LORA_OFF + DECAY_LORA + AAA_LORA
GN_EPS = 64e-5
LN_EPS = 1e-5
QK_SCALE = HEAD_DIM ** -0.5
LOG2_E = math.log2(math.e)

LANES = 128
KV_BLOCK = 256
QUERY_BLOCK = 1024
CHUNK = 64
GROUP_HEADS = 4
GROUP_W = GROUP_HEADS * HEAD_DIM
N_GROUPS = HEADS // GROUP_HEADS
BIAS_PARTS = 3
BIAS_STRIDE = 8
VMEM_LIMIT = 56 * 1024 * 1024


def _dot(a, b):
    return jnp.dot(a, b, preferred_element_type=F32)


def _dot_nt(a, b):
    return lax.dot_general(a, b, (((1,), (1,)), ((), ())), preferred_element_type=F32)


def _dot_tn(a, b):
    return lax.dot_general(a, b, (((0,), (0,)), ((), ())), preferred_element_type=F32)


def _layer_norm(x, g, b):
    mu = jnp.mean(x, -1, keepdims=True)
    xc = x - mu
    var = jnp.mean(xc * xc, -1, keepdims=True)
    return xc * lax.rsqrt(var + LN_EPS) * g + b


def _split3(x):
    p1 = x.astype(BF16)
    r1 = x - p1.astype(F32)
    p2 = r1.astype(BF16)
    p3 = (r1 - p2.astype(F32)).astype(BF16)
    return p1, p2, p3


def _dot3(m_bf16, x):
    p1, p2, p3 = _split3(x)
    return _dot(m_bf16, p1) + _dot(m_bf16, p2) + _dot(m_bf16, p3)


def _dot3_rhs(x, m_bf16):
    p1, p2, p3 = _split3(x)
    return _dot(p1, m_bf16) + _dot(p2, m_bf16) + _dot(p3, m_bf16)


def _seq_tri(n, seq_len, upper=False):
    row = lax.broadcasted_iota(jnp.int32, (n, n), 0)
    col = lax.broadcasted_iota(jnp.int32, (n, n), 1)
    keep = (row <= col) if upper else (col <= row)
    if seq_len < n:
        sh = int(math.log2(seq_len))
        assert 1 << sh == seq_len
        keep = keep & (lax.shift_right_logical(row, sh) == lax.shift_right_logical(col, sh))
    return jnp.where(keep, 1.0, 0.0).astype(BF16)


def _round_robin(gens, stagger):
    steps = [0] * len(gens)
    live, started = [], 0
    while started < len(gens) or live:
        if started < len(gens) and (started == 0 or steps[started - 1] >= stagger or (started - 1) not in live):
            live.append(started)
            started += 1
        for i in list(live):
            try:
                next(gens[i])
                steps[i] += 1
            except StopIteration:
                live.remove(i)


def _const_spec(shape):
    nd = len(shape)
    return pl.BlockSpec(shape, lambda *_: (0,) * nd, pipeline_mode=pl.Buffered(1))


def _params(sem):
    return pltpu.CompilerParams(dimension_semantics=sem, vmem_limit_bytes=VMEM_LIMIT)


def _bias_consts():
    e = np.zeros((2 * BIAS_PARTS, LANES, LANES), np.float32)
    ones_q = np.zeros((1, LANES), np.float32)
    ones_k = np.zeros((1, LANES), np.float32)
    for h in range(HEADS):
        for j in range(BIAS_PARTS):
            e[j, h, BIAS_STRIDE * h + j] = 1.0
            e[BIAS_PARTS + j, h, BIAS_STRIDE * h + BIAS_PARTS + j] = 1.0
            ones_q[0, BIAS_STRIDE * h + BIAS_PARTS + j] = 1.0
            ones_k[0, BIAS_STRIDE * h + j] = 1.0
    return jnp.asarray(e, BF16), jnp.asarray(ones_q), jnp.asarray(ones_k)


def _proj_kernel(x_ref, g_ref, b_ref, wq_ref, wkt_ref, wvt_ref, wft_ref, bf_ref, wg_ref,
                 e_ref, et_ref, oq_ref, ok_ref,
                 q_ref, kt_ref, vt_ref, ktb_ref, vtb_ref, lft_ref, cq_ref, ckt_ref, gt_ref,
                 carry_ref, *, seq_len, tm, logit_scale):
    xn = _layer_norm(x_ref[...], g_ref[...], b_ref[...]).astype(BF16)

    def transposed(w_ref, o32_ref, ob_ref):
        t = _dot_nt(w_ref[...], xn)
        o32_ref[0] = t
        for jb in range(tm // KV_BLOCK):
            ob_ref[jb] = t[:, jb * KV_BLOCK:(jb + 1) * KV_BLOCK].astype(BF16)

    rowi = lax.broadcasted_iota(jnp.int32, (LANES, tm), 0)
    lft = jax.nn.log_sigmoid(_dot_nt(wft_ref[...], xn) + bf_ref[...])
    lft = jnp.where(rowi < HEADS, lft, 0.0)
    lft_ref[0] = lft[:HEADS, :]
    q_ref[...] = (_dot(xn, wq_ref[...]) * (QK_SCALE * logit_scale)).astype(BF16)

    ct = _dot3_rhs(lft, _seq_tri(tm, seq_len, upper=True))
    if seq_len > tm:
        @pl.when(pl.program_id(0) % (seq_len // tm) == 0)
        def _():
            carry_ref[...] = jnp.zeros_like(carry_ref)
        ct = ct + carry_ref[...]
        carry_ref[...] = ct[:, tm - 1:tm]
    transposed(wkt_ref, kt_ref, ktb_ref)

    if logit_scale != 1.0:
        ct = ct * logit_scale
    c1t, c2t, c3t = _split3(ct)
    ckt = ok_ref[...] - (_dot(et_ref[0], c1t) + _dot(et_ref[1], c2t) + _dot(et_ref[2], c3t))
    for jb in range(tm // KV_BLOCK):
        ckt_ref[jb] = ckt[:, jb * KV_BLOCK:(jb + 1) * KV_BLOCK].astype(BF16)
    transposed(wvt_ref, vt_ref, vtb_ref)
    c1, c2, c3 = _split3(ct.T)
    cq = _dot(c1, e_ref[0]) + _dot(c2, e_ref[1]) + _dot(c3, e_ref[2]) + oq_ref[...]
    cq_ref[...] = cq.astype(BF16)
    gt_ref[...] = jax.nn.sigmoid(_dot(xn, wg_ref[...])).astype(BF16)


def _proj(x2d, ln_g, ln_b, wq, wkt, wvt, wft, bf_col, wg, seq_len, logit_scale):
    m, d = x2d.shape
    tm = min(512, m)
    assert m % tm == 0 and (seq_len % tm == 0 or tm % seq_len == 0) and tm % KV_BLOCK == 0
    e, ones_q, ones_k = _bias_consts()
    e_q = e[:BIAS_PARTS]
    e_kt = jnp.swapaxes(e[BIAS_PARTS:], 1, 2)
    ones_k_col = ones_k.reshape(LANES, 1)
    if seq_len >= tm:
        tps = seq_len // tm
        groups, cols = m // seq_len, seq_len
        tr = lambda r: pl.BlockSpec((1, r, tm), lambda i: (i // tps, 0, i % tps))
    else:
        groups, cols = m // tm, tm
        tr = lambda r: pl.BlockSpec((1, r, tm), lambda i: (i, 0, 0))
    nkb = tm // KV_BLOCK
    trb = lambda r: pl.BlockSpec((nkb, r, KV_BLOCK), lambda i: (i, 0, 0))
    row = lambda w: pl.BlockSpec((tm, w), lambda i: (i, 0))
    out_shape = (
        jax.ShapeDtypeStruct((m, WIDTH), BF16),
        jax.ShapeDtypeStruct((groups, WIDTH, cols), F32),
        jax.ShapeDtypeStruct((groups, WIDTH, cols), F32),
        jax.ShapeDtypeStruct((m // KV_BLOCK, WIDTH, KV_BLOCK), BF16),
        jax.ShapeDtypeStruct((m // KV_BLOCK, WIDTH, KV_BLOCK), BF16),
        jax.ShapeDtypeStruct((groups, HEADS, cols), F32),
        jax.ShapeDtypeStruct((m, LANES), BF16),
        jax.ShapeDtypeStruct((m // KV_BLOCK, LANES, KV_BLOCK), BF16),
        jax.ShapeDtypeStruct((m, 2 * d), BF16),
    )
    consts = (ln_g, ln_b, wq, wkt, wvt, wft, bf_col, wg, e_q, e_kt, ones_q, ones_k_col)
    return pl.pallas_call(
        functools.partial(_proj_kernel, seq_len=seq_len, tm=tm, logit_scale=logit_scale),
        out_shape=out_shape,
        grid=(m // tm,),
        in_specs=[row(d)] + [_const_spec(c.shape) for c in consts],
        out_specs=(row(WIDTH), tr(WIDTH), tr(WIDTH), trb(WIDTH), trb(WIDTH), tr(HEADS),
                   row(LANES), trb(LANES), row(2 * d)),
        scratch_shapes=[pltpu.VMEM((LANES, 1), F32)],
        compiler_params=_params(("arbitrary",)),
        name="proj",
    )(x2d, *consts)


def _group_mask(rows):
    r = lax.broadcasted_iota(jnp.int32, (rows, GROUP_W), 0)
    c = lax.broadcasted_iota(jnp.int32, (rows, GROUP_W), 1)
    sh = int(math.log2(HEAD_DIM))
    return (lax.shift_right_logical(r, sh) & (GROUP_HEADS - 1)) == lax.shift_right_logical(c, sh)


def _bd(y, mask):
    yb = y.astype(BF16)
    return jnp.where(mask, jnp.concatenate([yb] * GROUP_HEADS, axis=0), jnp.zeros((), BF16))


def _state_to_bd(st, mask_bd):
    out = []
    for g in range(N_GROUPS):
        blk = st[g * GROUP_W:(g + 1) * GROUP_W, :]
        out.append(jnp.where(mask_bd, jnp.concatenate([blk] * GROUP_HEADS, axis=1), 0.0))
    return out


def _state_from_bd(state, mask_bd):
    rows = []
    for sg in state:
        sm = jnp.where(mask_bd, sg, 0.0)
        acc = sm[:, 0:HEAD_DIM]
        for h in range(1, GROUP_HEADS):
            acc = acc + sm[:, h * HEAD_DIM:(h + 1) * HEAD_DIM]
        rows.append(acc)
    return jnp.concatenate(rows, axis=0)


def _rwkv_kernel(x_ref, g_ref, b_ref, wr_ref, mu_ref, w0_ref, w2_ref, a0_ref, a2_ref, g2_ref,
                 kk_ref, ka_ref, rk_ref, gng_ref, gnb_ref, ones_ref, shift_ref, state_ref,
                 o_ref, shift_out_ref, state_out_ref,
                 s_ref, prev_ref, *, tm, t_real, chained):
    n_sub = 2 if tm // CHUNK >= 4 else 1
    rows = tm // n_sub
    nc = rows // CHUNK
    mask_bd = _group_mask(GROUP_W)
    rowi = lax.broadcasted_iota(jnp.int32, (rows, 1), 0)
    tri3 = jnp.concatenate([_seq_tri(CHUNK, CHUNK)] * BIAS_PARTS, axis=1)
    last_rows = {}
    state_written = set()

    if chained:
        @pl.when(pl.program_id(1) == 0)
        def _():
            prev_ref[...] = shift_ref[0]
            for g, sg in enumerate(_state_to_bd(state_ref[0], mask_bd)):
                s_ref[g] = sg

    def prologue(si):
        r0, c0 = si * rows, si * nc
        x = x_ref[...].reshape(tm, x_ref.shape[-1])[r0:r0 + rows]
        xn = _layer_norm(x, g_ref[...], b_ref[...]).astype(BF16)
        p = _dot(xn, wr_ref[...])
        last_rows[si] = p[rows - 1:rows, :]
        yield
        if chained:
            prev = prev_ref[...] if si == 0 else last_rows[si - 1]
            p_prev = jnp.where(rowi == 0, prev, pltpu.roll(p, 1, 0))
            if si == n_sub - 1:
                prev_ref[...] = p[rows - 1:rows, :]
                shift_out_ref[0] = p[rows - 1:rows, :]
        else:
            first = jnp.concatenate([jnp.broadcast_to(shift_ref[c0 + c], (CHUNK, RWKV_PROJ)) for c in range(nc)],
                                    axis=0)
            p_prev = jnp.where((rowi & (CHUNK - 1)) == 0, first, pltpu.roll(p, 1, 0))
            for c in range(nc):
                shift_out_ref[c0 + c] = p[c * CHUNK + t_real - 1:c * CHUNK + t_real, :]
        pm = p + (p_prev - p) * mu_ref[...]
        r = pm[:, 0:WIDTH]
        k = pm[:, WIDTH:2 * WIDTH]
        v = pm[:, 2 * WIDTH:3 * WIDTH]
        dwa = pm[:, LORA_OFF:GATE_LORA_OFF]
        dg = pm[:, GATE_LORA_OFF:RWKV_PROJ]
        w_log = -jax.nn.softplus(-(w0_ref[...] + _dot(jnp.tanh(dwa).astype(BF16), w2_ref[...]))) - 0.5
        lw = -jnp.exp(w_log)
        a = jax.nn.sigmoid(a0_ref[...] + _dot(dwa.astype(BF16), a2_ref[...]))
        gate = _dot(jax.nn.sigmoid(dg).astype(BF16), g2_ref[...])
        yield
        kk = k * kk_ref[...]
        ss = _dot((kk * kk).astype(BF16), ones_ref[...])
        kk = kk * lax.rsqrt(jnp.maximum(ss, 1e-24))
        k = k * (1.0 + (a - 1.0) * ka_ref[...])
        bonus = _dot((r * k * rk_ref[...]).astype(BF16), ones_ref[...]) * v
        kka = kk * a
        if not chained and t_real < CHUNK:
            live = (rowi & (CHUNK - 1)) < t_real
            lw = jnp.where(live, lw, 0.0)
            r, k, v, kk, kka = (jnp.where(live, t, 0.0) for t in (r, k, v, kk, kka))
        yield
        lw_parts = _split3(lw)
        cl = jnp.concatenate(
            [_dot(tri3, jnp.concatenate([t[c * CHUNK:(c + 1) * CHUNK, :] for t in lw_parts], axis=0))
             for c in range(nc)], axis=0) if nc > 1 else _dot(tri3, jnp.concatenate(lw_parts, axis=0))
        e_ng = jnp.exp(-cl)
        return dict(rt=r * jnp.exp(cl), at=-kk * jnp.exp(cl - lw), bt=kka * e_ng, kt=k * e_ng,
                    v=v, k=k, kka=kka, cl=cl, bonus=bonus, gate=gate)

    row = lax.broadcasted_iota(jnp.int32, (CHUNK, GROUP_W), 0)
    col = lax.broadcasted_iota(jnp.int32, (CHUNK, GROUP_W), 1) & (CHUNK - 1)
    strict = col < row
    incl = col <= row
    blk8 = lax.shift_right_logical(row, 3) == lax.shift_right_logical(col, 3)
    blk16 = lax.shift_right_logical(row, 4) == lax.shift_right_logical(col, 4)
    blk32 = lax.shift_right_logical(row, 5) == lax.shift_right_logical(col, 5)

    def mm(x, y):
        return _dot(x.astype(BF16), _bd(y, mask_bd))

    probs = [(c, g) for c in range(nc) for g in range(N_GROUPS)]
    sl = lambda x, c, g: x[c * CHUNK:(c + 1) * CHUNK, g * GROUP_W:(g + 1) * GROUP_W]
    each = lambda f, *ls: [f(*xs) for xs in zip(*ls)]
    add = lambda *xs: functools.reduce(lambda a_, b_: a_ + b_, xs)
    stack = lambda a_, b_: jnp.concatenate([a_, b_], axis=0)
    top = lambda xs: [x[:CHUNK] for x in xs]
    bot = lambda xs: [x[CHUNK:] for x in xs]

    def scan(si, d):
        rt, at, bt, kt, v, k, kka, cl = (d[n] for n in ("rt", "at", "bt", "kt", "v", "k", "kka", "cl"))
        rt_, at_, bt_, kt_, v_ = ([sl(x, c, g) for c, g in probs] for x in (rt, at, bt, kt, v))
        cl_end = [cl[(c + 1) * CHUNK - 1:(c + 1) * CHUNK, :] for c in range(nc)]
        g_end = [jnp.exp(ce) for ce in cl_end]
        e_end = [jnp.exp(cl_end[c] - cl[c * CHUNK:(c + 1) * CHUNK, :]) for c in range(nc)]
        bh_ = [sl(kka, c, g) * e_end[c][:, g * GROUP_W:(g + 1) * GROUP_W] for c, g in probs]
        kh_ = [sl(k, c, g) * e_end[c][:, g * GROUP_W:(g + 1) * GROUP_W] for c, g in probs]
        yield
        ar_ = each(lambda a_, r_: jnp.concatenate([a_, r_], axis=0).astype(BF16), at_, rt_)
        gb_ = each(lambda ar, b_: _dot_nt(ar, _bd(b_, mask_bd)), ar_, bt_)
        yield
        gk_ = each(lambda ar, k_: _dot_nt(ar, _bd(k_, mask_bd)), ar_, kt_)
        n_ab = [jnp.where(strict, x[:CHUNK], 0.0) for x in gb_]
        a_ak = [jnp.where(strict, x[:CHUNK], 0.0) for x in gk_]
        p_rb = [jnp.where(incl, x[CHUNK:], 0.0) for x in gb_]
        p_rk = [jnp.where(incl, x[CHUNK:], 0.0) for x in gk_]
        yield
        uvp = each(mm, each(stack, a_ak, p_rk), v_)
        uv_, prkv = top(uvp), bot(uvp)
        yield
        n_d = [jnp.where(blk8, x, 0.0) for x in n_ab]
        n_0 = [jnp.where(blk16 & jnp.logical_not(blk8), x, 0.0) for x in n_ab]
        n_1 = [jnp.where(blk32 & jnp.logical_not(blk16), x, 0.0) for x in n_ab]
        n_2 = [jnp.where(blk32, 0.0, x) for x in n_ab]
        n2 = each(mm, n_d, n_d)
        yield
        n34 = each(mm, each(stack, n_d, n2), n2)
        n3, n4 = top(n34), bot(n34)
        q4 = each(add, n_d, n2, n3)
        yield
        t8 = each(add, q4, n4, each(mm, q4, n4))
        yield
        x0 = each(add, n_0, each(mm, n_0, t8))
        yield
        t16 = each(add, t8, x0, each(mm, t8, x0))
        yield
        x1 = each(add, n_1, each(mm, n_1, t16))
        yield
        t32 = each(add, t16, x1, each(mm, t16, x1))
        yield
        x2 = each(add, n_2, each(mm, n_2, t32))
        yield
        t_ = each(add, t32, x2, each(mm, t32, x2))
        yield
        wt_ = each(add, at_, each(mm, t_, at_))
        yield
        ut_ = each(add, uv_, each(mm, t_, uv_))
        yield
        mw_ = each(lambda w_, b_: jnp.where(mask_bd, _dot_tn(w_.astype(BF16), b_.astype(BF16)), 0.0).astype(BF16),
                   wt_, bh_)
        yield
        dd_ = each(lambda u_, vv, b_, k_: jnp.where(mask_bd, _dot_tn(
            jnp.concatenate([u_, vv], axis=0).astype(BF16), jnp.concatenate([b_, k_], axis=0).astype(BF16)), 0.0),
                   ut_, v_, bh_, kh_)
        yield
        qq_ = each(lambda r_, x: (r_ + x).astype(BF16), rt_, each(mm, p_rb, wt_))
        yield
        o0_ = each(add, each(mm, p_rb, ut_), prkv)
        yield

        if chained:
            while si > 0 and (si - 1) not in state_written:
                yield
            state = [s_ref[g] for g in range(N_GROUPS)]
        o_chunks = []
        for c in range(nc):
            if not chained:
                state = _state_to_bd(state_ref[si * nc + c], mask_bd)
            o_groups = []
            for g in range(N_GROUPS):
                j = c * N_GROUPS + g
                s0 = state[g]
                s0b = s0.astype(BF16)
                o_groups.append(_dot_nt(qq_[j], s0b) + o0_[j])
                state[g] = s0 * g_end[c][:, g * GROUP_W:(g + 1) * GROUP_W] + _dot(s0b, mw_[j]) + dd_[j]
            o_chunks.append(jnp.concatenate(o_groups, axis=1))
            if not chained:
                state_out_ref[si * nc + c] = _state_from_bd(state, mask_bd)
            yield
        if chained:
            for g in range(N_GROUPS):
                s_ref[g] = state[g]
            state_written.add(si)
        o = jnp.concatenate(o_chunks, axis=0) if len(o_chunks) > 1 else o_chunks[0]

        inv_n = 1.0 / HEAD_DIM
        mean = _dot(o.astype(BF16), ones_ref[...]) * inv_n
        oc = o - mean
        var = _dot((oc * oc).astype(BF16), ones_ref[...]) * inv_n
        on = oc * lax.rsqrt(var + GN_EPS) * gng_ref[...] + gnb_ref[...]
        out = ((on + d["bonus"]) * d["gate"]).astype(BF16)
        if chained:
            o_ref[0, si * rows:(si + 1) * rows, :] = out
        else:
            o_ref[si * nc:(si + 1) * nc] = out.reshape(nc, CHUNK, WIDTH)

    def pipeline(si):
        d = yield from prologue(si)
        yield
        yield from scan(si, d)

    _round_robin([pipeline(si) for si in range(n_sub)], stagger=4)

    if chained:
        @pl.when(pl.program_id(1) == pl.num_programs(1) - 1)
        def _():
            state_out_ref[0] = _state_from_bd([s_ref[g] for g in range(N_GROUPS)], mask_bd)


def _rwkv(x3d, t_real, ln_g, ln_b, wr, mu, w0, w2p, a0, a2p, g2, k_k, k_a, r_k, gn_g, gn_b,
          shift_in, state_in):
    nseq, t, d = x3d.shape
    chained = t_real == t
    if chained:
        tm, nb = min(512, t), 1
        assert t % tm == 0 and tm % CHUNK == 0
        grid = (nseq, t // tm)
        x_spec = pl.BlockSpec((1, tm, d), lambda b, i: (b, i, 0))
        o_spec = pl.BlockSpec((1, tm, WIDTH), lambda b, i: (b, i, 0))
    else:
        nb = min(8, nseq)
        tm = nb * CHUNK
        assert t == CHUNK and nseq % nb == 0
        grid = (nseq // nb, 1)
        x_spec = pl.BlockSpec((nb, CHUNK, d), lambda b, i: (b, 0, 0))
        o_spec = pl.BlockSpec((nb, CHUNK, WIDTH), lambda b, i: (b, 0, 0))
    ones_bd = jnp.asarray(np.kron(np.eye(HEADS, dtype=np.float32),
                                  np.ones((HEAD_DIM, HEAD_DIM), np.float32)), BF16)
    consts = (ln_g, ln_b, wr, mu, w0, w2p, a0, a2p, g2, k_k, k_a, r_k, gn_g, gn_b, ones_bd)
    state2d = state_in.reshape(nseq, HEADS * HEAD_DIM, HEAD_DIM)
    shift3d = shift_in.reshape(nseq, 1, RWKV_PROJ)
    seq_spec = lambda shape: pl.BlockSpec((nb,) + shape, lambda b, i: (b, 0, 0))
    o, shift_out, state_out = pl.pallas_call(
        functools.partial(_rwkv_kernel, tm=tm, t_real=t_real, chained=chained),
        out_shape=(jax.ShapeDtypeStruct((nseq, t, WIDTH), BF16),
                   jax.ShapeDtypeStruct((nseq, 1, RWKV_PROJ), F32),
                   jax.ShapeDtypeStruct((nseq, HEADS * HEAD_DIM, HEAD_DIM), F32)),
        grid=grid,
        in_specs=[x_spec] + [_const_spec(c.shape) for c in consts]
                 + [seq_spec((1, RWKV_PROJ)), seq_spec((HEADS * HEAD_DIM, HEAD_DIM))],
        out_specs=(o_spec, seq_spec((1, RWKV_PROJ)), seq_spec((HEADS * HEAD_DIM, HEAD_DIM))),
        scratch_shapes=[pltpu.VMEM((N_GROUPS, GROUP_W, GROUP_W), F32),
                        pltpu.VMEM((1, RWKV_PROJ), F32)],
        compiler_params=_params(("arbitrary", "arbitrary")),
        name="rwkv",
    )(x3d, *consts, shift3d, state2d)
    return (o, shift_out.reshape(nseq, RWKV_PROJ),
            state_out.reshape(nseq, HEADS, HEAD_DIM, HEAD_DIM))


def _softmax_step(carry, s, v_blk):
    m, l, acc = carry
    m_new = jnp.maximum(m, jnp.max(s, axis=-1, keepdims=True))
    alpha = jnp.exp(m - m_new)
    p = jnp.exp(s - m_new)
    l = alpha * l + jnp.sum(p, axis=-1, keepdims=True)
    acc = alpha * acc + _dot(p.astype(BF16), v_blk)
    return m_new, l, acc


def _fox_prompt_kernel(q_ref, cq_ref, k_ref, v_ref, ck_ref, o_ref, qa_ref, m_ref, acc_ref, *, tq):
    i = pl.program_id(1)
    lane = lax.broadcasted_iota(jnp.int32, (tq, LANES), 1)
    cq = cq_ref[0]
    zero = jnp.zeros((), BF16)
    for h in range(HEADS):
        q_pair = q_ref[0, :, (h // 2) * LANES:(h // 2 + 1) * LANES]
        q_h = jnp.where(lax.shift_right_logical(lane, 6) == (h % 2), q_pair, zero)
        c_h = jnp.where(lax.shift_right_logical(lane, 3) == h, cq, zero)
        qa_ref[h] = jnp.concatenate([q_h, c_h], axis=1)
    m_ref[...] = jnp.full(m_ref.shape, -jnp.inf, F32)
    acc_ref[...] = jnp.zeros(acc_ref.shape, F32)
    tk = KV_BLOCK
    ones = jnp.ones((LANES, tk), BF16)

    def block(j, r0, mask):
        ck = ck_ref[j]
        for pair in range(HEADS // 2):
            ls = slice(pair * LANES, (pair + 1) * LANES)
            ka = jnp.concatenate([k_ref[j, ls, :], ck], axis=0)
            v_aug = jnp.concatenate([v_ref[j, ls, :], ones], axis=0)
            for h in (2 * pair, 2 * pair + 1):
                s = _dot(qa_ref[h, r0:, :], ka)
                if mask is not None:
                    s = jnp.where(mask, s, -jnp.inf)
                parts = [s[:, t * LANES:(t + 1) * LANES] for t in range(tk // LANES)]
                m_old = m_ref[h, r0:, :]
                m_new = jnp.maximum(m_old, jnp.max(functools.reduce(jnp.maximum, parts), axis=-1, keepdims=True))
                alpha = jnp.exp2(m_old - m_new)
                p = jnp.concatenate([jnp.exp2(t - m_new) for t in parts], axis=1).astype(BF16)
                m_ref[h, r0:, :] = m_new
                acc_ref[h, r0:, :] = (jnp.concatenate([alpha, alpha], axis=1) * acc_ref[h, r0:, :]
                                      + _dot_nt(p, v_aug))

    def body(j, carry):
        block(j, 0, None)
        return carry

    n_full = i * (tq // tk)
    lax.fori_loop(0, n_full, body, 0)
    for d in range(tq // tk):
        rows = tq - d * tk
        row = lax.broadcasted_iota(jnp.int32, (rows, tk), 0)
        col = lax.broadcasted_iota(jnp.int32, (rows, tk), 1)
        block(n_full + d, d * tk, col <= row)
    for pair in range(HEADS // 2):
        lo = acc_ref[2 * pair, :, 0:LANES] / acc_ref[2 * pair, :, LANES:2 * LANES]
        hi = acc_ref[2 * pair + 1, :, 0:LANES] / acc_ref[2 * pair + 1, :, LANES:2 * LANES]
        o_ref[0, :, pair * LANES:(pair + 1) * LANES] = jnp.where(
            lax.shift_right_logical(lane, 6) == 0, lo, hi).astype(BF16)


def _fox_prompt(q, cq, ktb, vtb, ckt):
    b, s, _ = q.shape
    tq = min(QUERY_BLOCK, s)
    assert s % tq == 0 and tq % KV_BLOCK == 0
    nkb = s // KV_BLOCK
    blk = lambda w: pl.BlockSpec((1, tq, w), lambda bi, i: (bi, i, 0))
    full = lambda r: pl.BlockSpec((nkb, r, KV_BLOCK), lambda bi, i: (bi, 0, 0))
    return pl.pallas_call(
        functools.partial(_fox_prompt_kernel, tq=tq),
        out_shape=jax.ShapeDtypeStruct((b, s, WIDTH), BF16),
        grid=(b, s // tq),
        in_specs=[blk(WIDTH), blk(LANES), full(WIDTH), full(WIDTH), full(LANES)],
        out_specs=blk(WIDTH),
        scratch_shapes=[pltpu.VMEM((HEADS, tq, 2 * LANES), BF16),
                        pltpu.VMEM((HEADS, tq, LANES), F32),
                        pltpu.VMEM((HEADS, tq, 2 * LANES), F32)],
        compiler_params=_params(("arbitrary", "arbitrary")),
        name="fox_prompt",
    )(q, cq, ktb, vtb, ckt)


def _fox_sample_kernel(pt_ref, q_ref, kn_ref, vn_ref, ckn_ref, *refs, t_new, pps, page):
    k_refs = refs[0:pps]
    v_refs = refs[pps:2 * pps]
    lf_refs = refs[2 * pps:3 * pps]
    o_ref = refs[3 * pps]
    qrow_ref, m_ref, l_ref, acc_ref, run_ref, pad_ref = refs[3 * pps + 1:]
    del pt_ref
    step = pl.program_id(1)
    n_rows = t_new * HEADS
    rowh = lax.broadcasted_iota(jnp.int32, (n_rows, WIDTH), 0) & (HEADS - 1)
    laneh = lax.shift_right_logical(lax.broadcasted_iota(jnp.int32, (n_rows, WIDTH), 1), 6)
    head_mask = rowh == laneh

    @pl.when(step == 0)
    def _():
        q = q_ref[0]
        rows = jnp.concatenate([jnp.broadcast_to(q[t:t + 1, :], (HEADS, WIDTH)) for t in range(t_new)],
                               axis=0)
        qrow_ref[...] = jnp.where(head_mask, rows, 0.0).astype(BF16)
        m_ref[...] = jnp.full(m_ref.shape, -jnp.inf, F32)
        l_ref[...] = jnp.zeros(l_ref.shape, F32)
        acc_ref[...] = jnp.zeros(acc_ref.shape, F32)
        run_ref[...] = jnp.zeros(run_ref.shape, F32)

    qrows = qrow_ref[...]
    lf = jnp.concatenate([r[0] for r in lf_refs], axis=0)
    newer = lax.broadcasted_iota(jnp.int32, (page, page), 0) > lax.broadcasted_iota(jnp.int32, (page, page), 1)
    suf = _dot3_rhs(lf, jnp.where(newer, 1.0, 0.0).astype(BF16))
    run = run_ref[...]
    biases = []
    for n in range(pps):
        blk = slice(n * HEADS, (n + 1) * HEADS)
        biases.append(jnp.concatenate([suf[blk] + run] * t_new, axis=0))
        run = run + (suf[blk][:, 0:1] + lf[blk][:, 0:1])
    run_ref[...] = run
    pair = lambda refs, n: jnp.concatenate([refs[n][0].astype(BF16), refs[n + 1][0].astype(BF16)], axis=1)
    scores = [_dot(qrows, pair(k_refs, n)) + jnp.concatenate(biases[n:n + 2], axis=1)
              for n in range(0, pps, 2)]
    m_old = m_ref[...]
    m_new = jnp.maximum(m_old, jnp.max(functools.reduce(jnp.maximum, scores), axis=-1, keepdims=True))
    alpha = jnp.exp(m_old - m_new)
    ps = [jnp.exp(s - m_new) for s in scores]
    pv = functools.reduce(lambda a, b: a + b,
                          [_dot_nt(p.astype(BF16), pair(v_refs, 2 * i)) for i, p in enumerate(ps)])
    m_ref[...] = m_new
    l_ref[...] = alpha * l_ref[...] + jnp.sum(functools.reduce(lambda a, b: a + b, ps), axis=-1, keepdims=True)
    acc_ref[...] = alpha * acc_ref[...] + pv

    @pl.when(step == pl.num_programs(1) - 1)
    def _():
        pad_ref[...] = jnp.zeros(pad_ref.shape, F32)
        pad_ref[0, 0:t_new, :] = kn_ref[0]
        pad_ref[1, 0:t_new, :] = vn_ref[0]
        pad_ref[2, 0:t_new, 0:LANES] = ckn_ref[0]
        kn = pad_ref[0].astype(BF16)
        vn = pad_ref[1].astype(BF16)
        ckn = pad_ref[2, :, 0:LANES].astype(BF16)
        r8 = lax.broadcasted_iota(jnp.int32, (n_rows, LANES), 0) & (HEADS - 1)
        l8 = lax.broadcasted_iota(jnp.int32, (n_rows, LANES), 1)
        sel = (lax.shift_right_logical(l8, 3) == r8) & ((l8 & (BIAS_STRIDE - 1)) >= BIAS_PARTS) \
            & ((l8 & (BIAS_STRIDE - 1)) < 2 * BIAS_PARTS)
        cqr = jnp.where(sel, 1.0, 0.0).astype(BF16)
        s = _dot_nt(jnp.concatenate([qrows, cqr], axis=1), jnp.concatenate([kn, ckn], axis=1))
        key = lax.broadcasted_iota(jnp.int32, (n_rows, page), 1)
        tok = lax.shift_right_logical(lax.broadcasted_iota(jnp.int32, (n_rows, page), 0), 3)
        s = jnp.where(key <= tok, s, -jnp.inf)
        _, l, acc = _softmax_step((m_ref[...], l_ref[...], acc_ref[...]), s, vn)
        o = jnp.where(head_mask, acc / l, 0.0)
        o_ref[0] = jnp.sum(o.reshape(t_new, HEADS, WIDTH), axis=1).astype(BF16)


def _fox_sample(q, kb, vb, ck, cache_k, cache_v, cache_lf_t, page_table):
    db, t_new, _ = q.shape
    n_pages = page_table.shape[1]
    n_pool, _, page = cache_k.shape
    pps = 16
    assert n_pages % pps == 0 and pps % 2 == 0 and page == LANES and t_new <= HEADS
    n_rows = t_new * HEADS
    pt = page_table.reshape(-1)

    def page_idx(n):
        return lambda b, s, pt_ref: (pt_ref[b * n_pages + n_pages - 1 - (s * pps + n)], 0, 0)

    new = lambda w: pl.BlockSpec((1, t_new, w), lambda b, s, pt_ref: (b, 0, 0))
    in_specs = ([new(WIDTH), new(WIDTH), new(WIDTH), new(LANES)]
                + [pl.BlockSpec((1, WIDTH, page), page_idx(n)) for n in range(pps)]
                + [pl.BlockSpec((1, WIDTH, page), page_idx(n)) for n in range(pps)]
                + [pl.BlockSpec((1, HEADS, page), page_idx(n)) for n in range(pps)])
    grid_spec = pltpu.PrefetchScalarGridSpec(
        num_scalar_prefetch=1,
        grid=(db, n_pages // pps),
        in_specs=in_specs,
        out_specs=new(WIDTH),
        scratch_shapes=[pltpu.VMEM((n_rows, WIDTH), BF16),
                        pltpu.VMEM((n_rows, 1), F32), pltpu.VMEM((n_rows, 1), F32),
                        pltpu.VMEM((n_rows, WIDTH), F32), pltpu.VMEM((HEADS, 1), F32),
                        pltpu.VMEM((3, page, WIDTH), F32)],
    )
    return pl.pallas_call(
        functools.partial(_fox_sample_kernel, t_new=t_new, pps=pps, page=page),
        out_shape=jax.ShapeDtypeStruct((db, t_new, WIDTH), BF16),
        grid_spec=grid_spec,
        compiler_params=_params(("arbitrary", "arbitrary")),
        name="fox_sample",
    )(pt, q, kb, vb, ck, *([cache_k] * pps), *([cache_v] * pps), *([cache_lf_t] * pps))


def _mlp_kernel(x_ref, or_ref, of_ref, gt_ref, lg_ref, lb_ref, wb_ref, wo_ref, g1_ref, b1_ref,
                wu_ref, wd_ref, g2_ref, b2_ref, y_ref, *, alpha, ff_chunk):
    tm, d = x_ref.shape
    n_sub = 2 if tm >= 256 else 1
    rows = tm // n_sub

    def half(si):
        rs = slice(si * rows, (si + 1) * rows)
        hp = _layer_norm(x_ref[rs, :], lg_ref[...], lb_ref[...])
        gt = gt_ref[rs, :]
        yield
        merged = (gt[:, :d].astype(F32) * _dot(or_ref[rs, :], wb_ref[0:WIDTH, :])
                  + gt[:, d:].astype(F32) * _dot(of_ref[rs, :], wb_ref[WIDTH:2 * WIDTH, :]))
        yield
        pre = alpha * hp + _dot(merged.astype(BF16), wo_ref[...])
        yield
        x1 = _layer_norm(pre, g1_ref[...], b1_ref[...])
        x1b = x1.astype(BF16)
        yield
        h = jnp.zeros_like(x1)
        for c in range(wu_ref.shape[1] // ff_chunk):
            cs = slice(c * ff_chunk, (c + 1) * ff_chunk)
            up = jnp.maximum(_dot(x1b, wu_ref[:, cs]), 0.0)
            yield
            h = h + _dot((up * up).astype(BF16), wd_ref[cs, :])
            yield
        y_ref[rs, :] = _layer_norm(alpha * x1 + h, g2_ref[...], b2_ref[...])

    _round_robin([half(si) for si in range(n_sub)], stagger=4)


def _mlp(x2d, o_r, o_f, gates, ln_g, ln_b, wb, wo, g1, b1, wu, wd, g2, b2, alpha):
    m, d = x2d.shape
    tm = min(512, m)
    assert m % tm == 0
    row = lambda w: pl.BlockSpec((tm, w), lambda i: (i, 0))
    consts = (ln_g, ln_b, wb, wo, g1, b1, wu, wd, g2, b2)
    return pl.pallas_call(
        functools.partial(_mlp_kernel, alpha=alpha, ff_chunk=1024),
        out_shape=jax.ShapeDtypeStruct((m, d), F32),
        grid=(m // tm,),
        in_specs=[row(d), row(WIDTH), row(WIDTH), row(2 * d)] + [_const_spec(c.shape) for c in consts],
        out_specs=row(d),
        compiler_params=_params(("arbitrary",)),
        name="mlp",
    )(x2d, o_r, o_f, gates, *consts)


def kernel(x_prompt, x_sample, state_wkv, state_shift, cache_k, cache_v, cache_logf, page_table, ln_in_g, ln_in_b, w_in, rwkv_mu, rwkv_w0, rwkv_w2, rwkv_a0, rwkv_a2, rwkv_g2, rwkv_k_k, rwkv_k_a, rwkv_r_k, rwkv_gn_g, rwkv_gn_b, fox_b_f, w_branch, w_out, ln1_g, ln1_b, w_up, w_down, ln2_g, ln2_b):
    depth = w_in.shape[0]
    assert depth == 1, "the entry LayerNorm is fused into the first layer's kernels"
    bp, seq, d = x_prompt.shape
    db, dseq, _ = x_sample.shape
    n_pool, page = cache_k.shape[1], cache_k.shape[2]
    alpha = (2.0 * depth) ** 0.25
    row2 = lambda t: t.reshape(1, -1).astype(F32)
    ln_g, ln_b = row2(ln_in_g), row2(ln_in_b)

    l = 0
    w = w_in[l]
    fox_off = RWKV_PROJ
    wr = w[:, :RWKV_PROJ].astype(BF16)
    wq = w[:, fox_off:fox_off + WIDTH].astype(BF16)
    wt = w.T
    wkt = wt[fox_off + WIDTH:fox_off + 2 * WIDTH].astype(BF16)
    wvt = wt[fox_off + 2 * WIDTH:fox_off + 3 * WIDTH].astype(BF16)
    wft = jnp.pad(wt[fox_off + 3 * WIDTH:fox_off + 3 * WIDTH + HEADS], ((0, LANES - HEADS), (0, 0))).astype(BF16)
    wg = w[:, fox_off + 3 * WIDTH + HEADS:].astype(BF16)
    bf_col = jnp.pad(fox_b_f[l].reshape(HEADS, 1), ((0, LANES - HEADS), (0, 0))).astype(F32)
    proj_consts = (ln_g, ln_b, wq, wkt, wvt, wft, bf_col, wg)
    zl = jnp.zeros((DECAY_LORA, WIDTH), BF16)
    w2p = jnp.concatenate([rwkv_w2[l].astype(BF16), zl], axis=0)
    a2p = jnp.concatenate([zl, rwkv_a2[l].astype(BF16)], axis=0)
    rwkv_consts = (ln_g, ln_b, wr, row2(rwkv_mu[l]), row2(rwkv_w0[l]), w2p, row2(rwkv_a0[l]), a2p,
                   rwkv_g2[l].astype(BF16), row2(rwkv_k_k[l]), row2(rwkv_k_a[l]), row2(rwkv_r_k[l]),
                   row2(rwkv_gn_g[l]), row2(rwkv_gn_b[l]))
    mlp_consts = (ln_g, ln_b, w_branch[l].astype(BF16), w_out[l].astype(BF16), row2(ln1_g[l]),
                  row2(ln1_b[l]), w_up[l].astype(BF16), w_down[l].astype(BF16), row2(ln2_g[l]),
                  row2(ln2_b[l]))

    xp2 = x_prompt.reshape(bp * seq, d)
    q, kt32, vt32, ktb, vtb, lft, cq, ckt, gates = _proj(xp2, *proj_consts, seq, LOG2_E)
    to_seq = lambda t: t.reshape(bp, seq, t.shape[-1])
    o_f = _fox_prompt(to_seq(q), to_seq(cq), ktb, vtb, ckt)
    o_r, p_shift, p_wkv = _rwkv(x_prompt, seq, *rwkv_consts,
                                jnp.zeros((bp, RWKV_PROJ), F32),
                                jnp.zeros((bp, HEADS, HEAD_DIM, HEAD_DIM), F32))
    y_prompt = _mlp(xp2, o_r.reshape(bp * seq, WIDTH), o_f.reshape(bp * seq, WIDTH), gates,
                    *mlp_consts, alpha).reshape(bp, seq, d)
    heads_last = lambda t: jnp.transpose(t.reshape(bp, HEADS, HEAD_DIM, seq), (0, 3, 1, 2))[None]
    p_k = heads_last(kt32)
    p_v = heads_last(vt32)
    p_lf = jnp.swapaxes(lft, 1, 2)[None]

    xs2 = x_sample.reshape(db * dseq, d)
    q, kt32, vt32, _, _, lft, cq, ckt, gates = _proj(xs2, *proj_consts, dseq, 1.0)
    rows = lambda t: jnp.swapaxes(t, 1, 2).reshape(db * dseq, t.shape[1])
    k32, v32, lf, ck = rows(kt32), rows(vt32), rows(lft), rows(ckt)
    to_seq = lambda t: t.reshape(db, dseq, t.shape[-1])
    o_f = _fox_sample(to_seq(q.astype(F32)), to_seq(k32), to_seq(v32), to_seq(ck.astype(F32)),
                      jnp.transpose(cache_k[l], (0, 2, 3, 1)).reshape(n_pool, WIDTH, page),
                      jnp.transpose(cache_v[l], (0, 2, 3, 1)).reshape(n_pool, WIDTH, page),
                      jnp.swapaxes(cache_logf[l], 1, 2), page_table)
    xs_pad = jnp.pad(x_sample, ((0, 0), (0, CHUNK - dseq), (0, 0)))
    o_r, s_shift, s_wkv = _rwkv(xs_pad, dseq, *rwkv_consts, state_shift[l], state_wkv[l])
    y_sample = _mlp(xs2, o_r[:, :dseq].reshape(db * dseq, WIDTH), o_f.reshape(db * dseq, WIDTH), gates,
                    *mlp_consts, alpha).reshape(db, dseq, d)
    s_k = k32.reshape(1, db, dseq, HEADS, HEAD_DIM)
    s_v = v32.reshape(1, db, dseq, HEADS, HEAD_DIM)
    s_lf = lf.reshape(1, db, dseq, HEADS)

    return (y_prompt, y_sample, p_wkv[None], p_shift[None], p_k, p_v, p_lf,
            s_wkv[None], s_shift[None], s_k, s_v, s_lf)
```

```python
import functools
import math

import numpy as np
import jax
import jax.numpy as jnp
from jax import lax
from jax.experimental import pallas as pl
from jax.experimental.pallas import tpu as pltpu

F32 = jnp.float32
BF16 = jnp.bfloat16

HEADS = 8
HEAD_DIM = 64
WIDTH = HEADS * HEAD_DIM
DECAY_LORA = 64
AAA_LORA = 64
GATE_LORA = 128
RWKV_PROJ = 3 * WIDTH + DECAY_LORA + AAA_LORA + GATE_LORA
LORA_OFF = 3 * WIDTH
GATE_LORA_OFF = LORA_OFF + DECAY_LORA + AAA_LORA
GN_EPS = 64e-5
LN_EPS = 1e-5
QK_SCALE = HEAD_DIM ** -0.5
LOG2_E = math.log2(math.e)

LANES = 128
KV_BLOCK = 256
QUERY_BLOCK = 1024
CHUNK = 64
GROUP_HEADS = 4
GROUP_W = GROUP_HEADS * HEAD_DIM
N_GROUPS = HEADS // GROUP_HEADS
BIAS_PARTS = 3
BIAS_STRIDE = 8
VMEM_LIMIT = 56 * 1024 * 1024


def _dot(a, b):
    return jnp.dot(a, b, preferred_element_type=F32)


def _dot_nt(a, b):
    return lax.dot_general(a, b, (((1,), (1,)), ((), ())), preferred_element_type=F32)


def _dot_tn(a, b):
    return lax.dot_general(a, b, (((0,), (0,)), ((), ())), preferred_element_type=F32)


def _layer_norm(x, g, b):
    mu = jnp.mean(x, -1, keepdims=True)
    xc = x - mu
    var = jnp.mean(xc * xc, -1, keepdims=True)
    return xc * lax.rsqrt(var + LN_EPS) * g + b


def _split3(x):
    p1 = x.astype(BF16)
    r1 = x - p1.astype(F32)
    p2 = r1.astype(BF16)
    p3 = (r1 - p2.astype(F32)).astype(BF16)
    return p1, p2, p3


def _dot3(m_bf16, x):
    p1, p2, p3 = _split3(x)
    return _dot(m_bf16, p1) + _dot(m_bf16, p2) + _dot(m_bf16, p3)


def _dot3_rhs(x, m_bf16):
    p1, p2, p3 = _split3(x)
    return _dot(p1, m_bf16) + _dot(p2, m_bf16) + _dot(p3, m_bf16)


def _seq_tri(n, seq_len, upper=False):
    row = lax.broadcasted_iota(jnp.int32, (n, n), 0)
    col = lax.broadcasted_iota(jnp.int32, (n, n), 1)
    keep = (row <= col) if upper else (col <= row)
    if seq_len < n:
        sh = int(math.log2(seq_len))
        assert 1 << sh == seq_len
        keep = keep & (lax.shift_right_logical(row, sh) == lax.shift_right_logical(col, sh))
    return jnp.where(keep, 1.0, 0.0).astype(BF16)


def _round_robin(gens, stagger):
    steps = [0] * len(gens)
    live, started = [], 0
    while started < len(gens) or live:
        if started < len(gens) and (started == 0 or steps[started - 1] >= stagger or (started - 1) not in live):
            live.append(started)
            started += 1
        for i in list(live):
            try:
                next(gens[i])
                steps[i] += 1
            except StopIteration:
                live.remove(i)


def _const_spec(shape):
    nd = len(shape)
    return pl.BlockSpec(shape, lambda *_: (0,) * nd, pipeline_mode=pl.Buffered(1))


def _params(sem):
    return pltpu.CompilerParams(dimension_semantics=sem, vmem_limit_bytes=VMEM_LIMIT)


def _bias_consts():
    e = np.zeros((2 * BIAS_PARTS, LANES, LANES), np.float32)
    ones_q = np.zeros((1, LANES), np.float32)
    ones_k = np.zeros((1, LANES), np.float32)
    for h in range(HEADS):
        for j in range(BIAS_PARTS):
            e[j, h, BIAS_STRIDE * h + j] = 1.0
            e[BIAS_PARTS + j, h, BIAS_STRIDE * h + BIAS_PARTS + j] = 1.0
            ones_q[0, BIAS_STRIDE * h + BIAS_PARTS + j] = 1.0
            ones_k[0, BIAS_STRIDE * h + j] = 1.0
    return jnp.asarray(e, BF16), jnp.asarray(ones_q), jnp.asarray(ones_k)


def _proj_kernel(x_ref, g_ref, b_ref, wq_ref, wkt_ref, wvt_ref, wft_ref, bf_ref, wg_ref,
                 e_ref, et_ref, oq_ref, ok_ref,
                 q_ref, kt_ref, vt_ref, ktb_ref, vtb_ref, lft_ref, cq_ref, ckt_ref, gt_ref,
                 carry_ref, *, seq_len, tm, logit_scale):
    xn = _layer_norm(x_ref[...], g_ref[...], b_ref[...]).astype(BF16)

    def transposed(w_ref, o32_ref, ob_ref):
        t = _dot_nt(w_ref[...], xn)
        o32_ref[0] = t
        for jb in range(tm // KV_BLOCK):
            ob_ref[jb] = t[:, jb * KV_BLOCK:(jb + 1) * KV_BLOCK].astype(BF16)

    rowi = lax.broadcasted_iota(jnp.int32, (LANES, tm), 0)
    lft = jax.nn.log_sigmoid(_dot_nt(wft_ref[...], xn) + bf_ref[...])
    lft = jnp.where(rowi < HEADS, lft, 0.0)
    lft_ref[0] = lft[:HEADS, :]
    q_ref[...] = (_dot(xn, wq_ref[...]) * (QK_SCALE * logit_scale)).astype(BF16)

    ct = _dot3_rhs(lft, _seq_tri(tm, seq_len, upper=True))
    if seq_len > tm:
        @pl.when(pl.program_id(0) % (seq_len // tm) == 0)
        def _():
            carry_ref[...] = jnp.zeros_like(carry_ref)
        ct = ct + carry_ref[...]
        carry_ref[...] = ct[:, tm - 1:tm]
    transposed(wkt_ref, kt_ref, ktb_ref)

    if logit_scale != 1.0:
        ct = ct * logit_scale
    c1t, c2t, c3t = _split3(ct)
    ckt = ok_ref[...] - (_dot(et_ref[0], c1t) + _dot(et_ref[1], c2t) + _dot(et_ref[2], c3t))
    for jb in range(tm // KV_BLOCK):
        ckt_ref[jb] = ckt[:, jb * KV_BLOCK:(jb + 1) * KV_BLOCK].astype(BF16)
    transposed(wvt_ref, vt_ref, vtb_ref)
    c1, c2, c3 = _split3(ct.T)
    cq = _dot(c1, e_ref[0]) + _dot(c2, e_ref[1]) + _dot(c3, e_ref[2]) + oq_ref[...]
    cq_ref[...] = cq.astype(BF16)
    gt_ref[...] = jax.nn.sigmoid(_dot(xn, wg_ref[...])).astype(BF16)


def _proj(x2d, ln_g, ln_b, wq, wkt, wvt, wft, bf_col, wg, seq_len, logit_scale):
    m, d = x2d.shape
    tm = min(512, m)
    assert m % tm == 0 and (seq_len % tm == 0 or tm % seq_len == 0) and tm % KV_BLOCK == 0
    e, ones_q, ones_k = _bias_consts()
    e_q = e[:BIAS_PARTS]
    e_kt = jnp.swapaxes(e[BIAS_PARTS:], 1, 2)
    ones_k_col = ones_k.reshape(LANES, 1)
    if seq_len >= tm:
        tps = seq_len // tm
        groups, cols = m // seq_len, seq_len
        tr = lambda r: pl.BlockSpec((1, r, tm), lambda i: (i // tps, 0, i % tps))
    else:
        groups, cols = m // tm, tm
        tr = lambda r: pl.BlockSpec((1, r, tm), lambda i: (i, 0, 0))
    nkb = tm // KV_BLOCK
    trb = lambda r: pl.BlockSpec((nkb, r, KV_BLOCK), lambda i: (i, 0, 0))
    row = lambda w: pl.BlockSpec((tm, w), lambda i: (i, 0))
    out_shape = (
        jax.ShapeDtypeStruct((m, WIDTH), BF16),
        jax.ShapeDtypeStruct((groups, WIDTH, cols), F32),
        jax.ShapeDtypeStruct((groups, WIDTH, cols), F32),
        jax.ShapeDtypeStruct((m // KV_BLOCK, WIDTH, KV_BLOCK), BF16),
        jax.ShapeDtypeStruct((m // KV_BLOCK, WIDTH, KV_BLOCK), BF16),
        jax.ShapeDtypeStruct((groups, HEADS, cols), F32),
        jax.ShapeDtypeStruct((m, LANES), BF16),
        jax.ShapeDtypeStruct((m // KV_BLOCK, LANES, KV_BLOCK), BF16),
        jax.ShapeDtypeStruct((m, 2 * d), BF16),
    )
    consts = (ln_g, ln_b, wq, wkt, wvt, wft, bf_col, wg, e_q, e_kt, ones_q, ones_k_col)
    return pl.pallas_call(
        functools.partial(_proj_kernel, seq_len=seq_len, tm=tm, logit_scale=logit_scale),
        out_shape=out_shape,
        grid=(m // tm,),
        in_specs=[row(d)] + [_const_spec(c.shape) for c in consts],
        out_specs=(row(WIDTH), tr(WIDTH), tr(WIDTH), trb(WIDTH), trb(WIDTH), tr(HEADS),
                   row(LANES), trb(LANES), row(2 * d)),
        scratch_shapes=[pltpu.VMEM((LANES, 1), F32)],
        compiler_params=_params(("arbitrary",)),
        name="proj",
    )(x2d, *consts)


def _group_mask(rows):
    r = lax.broadcasted_iota(jnp.int32, (rows, GROUP_W), 0)
    c = lax.broadcasted_iota(jnp.int32, (rows, GROUP_W), 1)
    sh = int(math.log2(HEAD_DIM))
    return (lax.shift_right_logical(r, sh) & (GROUP_HEADS - 1)) == lax.shift_right_logical(c, sh)


def _bd(y, mask):
    yb = y.astype(BF16)
    return jnp.where(mask, jnp.concatenate([yb] * GROUP_HEADS, axis=0), jnp.zeros((), BF16))


def _state_to_bd(st, mask_bd):
    out = []
    for g in range(N_GROUPS):
        blk = st[g * GROUP_W:(g + 1) * GROUP_W, :]
        out.append(jnp.where(mask_bd, jnp.concatenate([blk] * GROUP_HEADS, axis=1), 0.0))
    return out


def _state_from_bd(state, mask_bd):
    rows = []
    for sg in state:
        sm = jnp.where(mask_bd, sg, 0.0)
        acc = sm[:, 0:HEAD_DIM]
        for h in range(1, GROUP_HEADS):
            acc = acc + sm[:, h * HEAD_DIM:(h + 1) * HEAD_DIM]
        rows.append(acc)
    return jnp.concatenate(rows, axis=0)


def _rwkv_kernel(x_ref, g_ref, b_ref, wr_ref, mu_ref, w0_ref, w2_ref, a0_ref, a2_ref, g2_ref,
                 kk_ref, ka_ref, rk_ref, gng_ref, gnb_ref, ones_ref, shift_ref, state_ref,
                 o_ref, shift_out_ref, state_out_ref,
                 s_ref, prev_ref, *, tm, t_real, chained):
    n_sub = 2 if tm // CHUNK >= 4 else 1
    rows = tm // n_sub
    nc = rows // CHUNK
    mask_bd = _group_mask(GROUP_W)
    rowi = lax.broadcasted_iota(jnp.int32, (rows, 1), 0)
    tri3 = jnp.concatenate([_seq_tri(CHUNK, CHUNK)] * BIAS_PARTS, axis=1)
    last_rows = {}
    state_written = set()

    if chained:
        @pl.when(pl.program_id(1) == 0)
        def _():
            prev_ref[...] = shift_ref[0]
            for g, sg in enumerate(_state_to_bd(state_ref[0], mask_bd)):
                s_ref[g] = sg

    def prologue(si):
        r0, c0 = si * rows, si * nc
        x = x_ref[...].reshape(tm, x_ref.shape[-1])[r0:r0 + rows]
        xn = _layer_norm(x, g_ref[...], b_ref[...]).astype(BF16)
        p = _dot(xn, wr_ref[...])
        last_rows[si] = p[rows - 1:rows, :]
        yield
        if chained:
            prev = prev_ref[...] if si == 0 else last_rows[si - 1]
            p_prev = jnp.where(rowi == 0, prev, pltpu.roll(p, 1, 0))
            if si == n_sub - 1:
                prev_ref[...] = p[rows - 1:rows, :]
                shift_out_ref[0] = p[rows - 1:rows, :]
        else:
            first = jnp.concatenate([jnp.broadcast_to(shift_ref[c0 + c], (CHUNK, RWKV_PROJ)) for c in range(nc)],
                                    axis=0)
            p_prev = jnp.where((rowi & (CHUNK - 1)) == 0, first, pltpu.roll(p, 1, 0))
            for c in range(nc):
                shift_out_ref[c0 + c] = p[c * CHUNK + t_real - 1:c * CHUNK + t_real, :]
        pm = p + (p_prev - p) * mu_ref[...]
        r = pm[:, 0:WIDTH]
        k = pm[:, WIDTH:2 * WIDTH]
        v = pm[:, 2 * WIDTH:3 * WIDTH]
        dwa = pm[:, LORA_OFF:GATE_LORA_OFF]
        dg = pm[:, GATE_LORA_OFF:RWKV_PROJ]
        w_log = -jax.nn.softplus(-(w0_ref[...] + _dot(jnp.tanh(dwa).astype(BF16), w2_ref[...]))) - 0.5
        lw = -jnp.exp(w_log)
        a = jax.nn.sigmoid(a0_ref[...] + _dot(dwa.astype(BF16), a2_ref[...]))
        gate = _dot(jax.nn.sigmoid(dg).astype(BF16), g2_ref[...])
        yield
        kk = k * kk_ref[...]
        ss = _dot((kk * kk).astype(BF16), ones_ref[...])
        kk = kk * lax.rsqrt(jnp.maximum(ss, 1e-24))
        k = k * (1.0 + (a - 1.0) * ka_ref[...])
        bonus = _dot((r * k * rk_ref[...]).astype(BF16), ones_ref[...]) * v
        kka = kk * a
        if not chained and t_real < CHUNK:
            live = (rowi & (CHUNK - 1)) < t_real
            lw = jnp.where(live, lw, 0.0)
            r, k, v, kk, kka = (jnp.where(live, t, 0.0) for t in (r, k, v, kk, kka))
        yield
        lw_parts = _split3(lw)
        cl = jnp.concatenate(
            [_dot(tri3, jnp.concatenate([t[c * CHUNK:(c + 1) * CHUNK, :] for t in lw_parts], axis=0))
             for c in range(nc)], axis=0) if nc > 1 else _dot(tri3, jnp.concatenate(lw_parts, axis=0))
        e_ng = jnp.exp(-cl)
        return dict(rt=r * jnp.exp(cl), at=-kk * jnp.exp(cl - lw), bt=kka * e_ng, kt=k * e_ng,
                    v=v, k=k, kka=kka, cl=cl, bonus=bonus, gate=gate)

    row = lax.broadcasted_iota(jnp.int32, (CHUNK, GROUP_W), 0)
    col = lax.broadcasted_iota(jnp.int32, (CHUNK, GROUP_W), 1) & (CHUNK - 1)
    strict = col < row
    incl = col <= row
    blk8 = lax.shift_right_logical(row, 3) == lax.shift_right_logical(col, 3)
    blk16 = lax.shift_right_logical(row, 4) == lax.shift_right_logical(col, 4)
    blk32 = lax.shift_right_logical(row, 5) == lax.shift_right_logical(col, 5)

    def mm(x, y):
        return _dot(x.astype(BF16), _bd(y, mask_bd))

    probs = [(c, g) for c in range(nc) for g in range(N_GROUPS)]
    sl = lambda x, c, g: x[c * CHUNK:(c + 1) * CHUNK, g * GROUP_W:(g + 1) * GROUP_W]
    each = lambda f, *ls: [f(*xs) for xs in zip(*ls)]
    add = lambda *xs: functools.reduce(lambda a_, b_: a_ + b_, xs)
    stack = lambda a_, b_: jnp.concatenate([a_, b_], axis=0)
    top = lambda xs: [x[:CHUNK] for x in xs]
    bot = lambda xs: [x[CHUNK:] for x in xs]

    def scan(si, d):
        rt, at, bt, kt, v, k, kka, cl = (d[n] for n in ("rt", "at", "bt", "kt", "v", "k", "kka", "cl"))
        rt_, at_, bt_, kt_, v_ = ([sl(x, c, g) for c, g in probs] for x in (rt, at, bt, kt, v))
        cl_end = [cl[(c + 1) * CHUNK - 1:(c + 1) * CHUNK, :] for c in range(nc)]
        g_end = [jnp.exp(ce) for ce in cl_end]
        e_end = [jnp.exp(cl_end[c] - cl[c * CHUNK:(c + 1) * CHUNK, :]) for c in range(nc)]
        bh_ = [sl(kka, c, g) * e_end[c][:, g * GROUP_W:(g + 1) * GROUP_W] for c, g in probs]
        kh_ = [sl(k, c, g) * e_end[c][:, g * GROUP_W:(g + 1) * GROUP_W] for c, g in probs]
        yield
        ar_ = each(lambda a_, r_: jnp.concatenate([a_, r_], axis=0).astype(BF16), at_, rt_)
        gb_ = each(lambda ar, b_: _dot_nt(ar, _bd(b_, mask_bd)), ar_, bt_)
        yield
        gk_ = each(lambda ar, k_: _dot_nt(ar, _bd(k_, mask_bd)), ar_, kt_)
        n_ab = [jnp.where(strict, x[:CHUNK], 0.0) for x in gb_]
        a_ak = [jnp.where(strict, x[:CHUNK], 0.0) for x in gk_]
        p_rb = [jnp.where(incl, x[CHUNK:], 0.0) for x in gb_]
        p_rk = [jnp.where(incl, x[CHUNK:], 0.0) for x in gk_]
        yield
        uvp = each(mm, each(stack, a_ak, p_rk), v_)
        uv_, prkv = top(uvp), bot(uvp)
        yield
        n_d = [jnp.where(blk8, x, 0.0) for x in n_ab]
        n_0 = [jnp.where(blk16 & jnp.logical_not(blk8), x, 0.0) for x in n_ab]
        n_1 = [jnp.where(blk32 & jnp.logical_not(blk16), x, 0.0) for x in n_ab]
        n_2 = [jnp.where(blk32, 0.0, x) for x in n_ab]
        n2 = each(mm, n_d, n_d)
        yield
        n34 = each(mm, each(stack, n_d, n2), n2)
        n3, n4 = top(n34), bot(n34)
        q4 = each(add, n_d, n2, n3)
        yield
        t8 = each(add, q4, n4, each(mm, q4, n4))
        yield
        x0 = each(add, n_0, each(mm, n_0, t8))
        yield
        t16 = each(add, t8, x0, each(mm, t8, x0))
        yield
        x1 = each(add, n_1, each(mm, n_1, t16))
        yield
        t32 = each(add, t16, x1, each(mm, t16, x1))
        yield
        x2 = each(add, n_2, each(mm, n_2, t32))
        yield
        t_ = each(add, t32, x2, each(mm, t32, x2))
        yield
        wt_ = each(add, at_, each(mm, t_, at_))
        yield
        ut_ = each(add, uv_, each(mm, t_, uv_))
        yield
        mw_ = each(lambda w_, b_: jnp.where(mask_bd, _dot_tn(w_.astype(BF16), b_.astype(BF16)), 0.0).astype(BF16),
                   wt_, bh_)
        yield
        dd_ = each(lambda u_, vv, b_, k_: jnp.where(mask_bd, _dot_tn(
            jnp.concatenate([u_, vv], axis=0).astype(BF16), jnp.concatenate([b_, k_], axis=0).astype(BF16)), 0.0),
                   ut_, v_, bh_, kh_)
        yield
        qq_ = each(lambda r_, x: (r_ + x).astype(BF16), rt_, each(mm, p_rb, wt_))
        yield
        o0_ = each(add, each(mm, p_rb, ut_), prkv)
        yield

        if chained:
            while si > 0 and (si - 1) not in state_written:
                yield
            state = [s_ref[g] for g in range(N_GROUPS)]
        o_chunks = []
        for c in range(nc):
            if not chained:
                state = _state_to_bd(state_ref[si * nc + c], mask_bd)
            o_groups = []
            for g in range(N_GROUPS):
                j = c * N_GROUPS + g
                s0 = state[g]
                s0b = s0.astype(BF16)
                o_groups.append(_dot_nt(qq_[j], s0b) + o0_[j])
                state[g] = s0 * g_end[c][:, g * GROUP_W:(g + 1) * GROUP_W] + _dot(s0b, mw_[j]) + dd_[j]
            o_chunks.append(jnp.concatenate(o_groups, axis=1))
            if not chained:
                state_out_ref[si * nc + c] = _state_from_bd(state, mask_bd)
            yield
        if chained:
            for g in range(N_GROUPS):
                s_ref[g] = state[g]
            state_written.add(si)
        o = jnp.concatenate(o_chunks, axis=0) if len(o_chunks) > 1 else o_chunks[0]

        inv_n = 1.0 / HEAD_DIM
        mean = _dot(o.astype(BF16), ones_ref[...]) * inv_n
        oc = o - mean
        var = _dot((oc * oc).astype(BF16), ones_ref[...]) * inv_n
        on = oc * lax.rsqrt(var + GN_EPS) * gng_ref[...] + gnb_ref[...]
        out = ((on + d["bonus"]) * d["gate"]).astype(BF16)
        if chained:
            o_ref[0, si * rows:(si + 1) * rows, :] = out
        else:
            o_ref[si * nc:(si + 1) * nc] = out.reshape(nc, CHUNK, WIDTH)

    def pipeline(si):
        d = yield from prologue(si)
        yield
        yield from scan(si, d)

    _round_robin([pipeline(si) for si in range(n_sub)], stagger=4)

    if chained:
        @pl.when(pl.program_id(1) == pl.num_programs(1) - 1)
        def _():
            state_out_ref[0] = _state_from_bd([s_ref[g] for g in range(N_GROUPS)], mask_bd)


def _rwkv(x3d, t_real, ln_g, ln_b, wr, mu, w0, w2p, a0, a2p, g2, k_k, k_a, r_k, gn_g, gn_b,
          shift_in, state_in):
    nseq, t, d = x3d.shape
    chained = t_real == t
    if chained:
        tm, nb = min(512, t), 1
        assert t % tm == 0 and tm % CHUNK == 0
        grid = (nseq, t // tm)
        x_spec = pl.BlockSpec((1, tm, d), lambda b, i: (b, i, 0))
        o_spec = pl.BlockSpec((1, tm, WIDTH), lambda b, i: (b, i, 0))
    else:
        nb = min(8, nseq)
        tm = nb * CHUNK
        assert t == CHUNK and nseq % nb == 0
        grid = (nseq // nb, 1)
        x_spec = pl.BlockSpec((nb, CHUNK, d), lambda b, i: (b, 0, 0))
        o_spec = pl.BlockSpec((nb, CHUNK, WIDTH), lambda b, i: (b, 0, 0))
    ones_bd = jnp.asarray(np.kron(np.eye(HEADS, dtype=np.float32),
                                  np.ones((HEAD_DIM, HEAD_DIM), np.float32)), BF16)
    consts = (ln_g, ln_b, wr, mu, w0, w2p, a0, a2p, g2, k_k, k_a, r_k, gn_g, gn_b, ones_bd)
    state2d = state_in.reshape(nseq, HEADS * HEAD_DIM, HEAD_DIM)
    shift3d = shift_in.reshape(nseq, 1, RWKV_PROJ)
    seq_spec = lambda shape: pl.BlockSpec((nb,) + shape, lambda b, i: (b, 0, 0))
    o, shift_out, state_out = pl.pallas_call(
        functools.partial(_rwkv_kernel, tm=tm, t_real=t_real, chained=chained),
        out_shape=(jax.ShapeDtypeStruct((nseq, t, WIDTH), BF16),
                   jax.ShapeDtypeStruct((nseq, 1, RWKV_PROJ), F32),
                   jax.ShapeDtypeStruct((nseq, HEADS * HEAD_DIM, HEAD_DIM), F32)),
        grid=grid,
        in_specs=[x_spec] + [_const_spec(c.shape) for c in consts]
                 + [seq_spec((1, RWKV_PROJ)), seq_spec((HEADS * HEAD_DIM, HEAD_DIM))],
        out_specs=(o_spec, seq_spec((1, RWKV_PROJ)), seq_spec((HEADS * HEAD_DIM, HEAD_DIM))),
        scratch_shapes=[pltpu.VMEM((N_GROUPS, GROUP_W, GROUP_W), F32),
                        pltpu.VMEM((1, RWKV_PROJ), F32)],
        compiler_params=_params(("arbitrary", "arbitrary")),
        name="rwkv",
    )(x3d, *consts, shift3d, state2d)
    return (o, shift_out.reshape(nseq, RWKV_PROJ),
            state_out.reshape(nseq, HEADS, HEAD_DIM, HEAD_DIM))


def _softmax_step(carry, s, v_blk):
    m, l, acc = carry
    m_new = jnp.maximum(m, jnp.max(s, axis=-1, keepdims=True))
    alpha = jnp.exp(m - m_new)
    p = jnp.exp(s - m_new)
    l = alpha * l + jnp.sum(p, axis=-1, keepdims=True)
    acc = alpha * acc + _dot(p.astype(BF16), v_blk)
    return m_new, l, acc


def _fox_prompt_kernel(q_ref, cq_ref, k_ref, v_ref, ck_ref, o_ref, qa_ref, m_ref, acc_ref, *, tq):
    i = pl.program_id(1)
    lane = lax.broadcasted_iota(jnp.int32, (tq, LANES), 1)
    cq = cq_ref[0]
    zero = jnp.zeros((), BF16)
    for h in range(HEADS):
        q_pair = q_ref[0, :, (h // 2) * LANES:(h // 2 + 1) * LANES]
        q_h = jnp.where(lax.shift_right_logical(lane, 6) == (h % 2), q_pair, zero)
        c_h = jnp.where(lax.shift_right_logical(lane, 3) == h, cq, zero)
        qa_ref[h] = jnp.concatenate([q_h, c_h], axis=1)
    m_ref[...] = jnp.full(m_ref.shape, -jnp.inf, F32)
    acc_ref[...] = jnp.zeros(acc_ref.shape, F32)
    tk = KV_BLOCK
    ones = jnp.ones((LANES, tk), BF16)

    def block(j, r0, mask):
        ck = ck_ref[j]
        for pair in range(HEADS // 2):
            ls = slice(pair * LANES, (pair + 1) * LANES)
            ka = jnp.concatenate([k_ref[j, ls, :], ck], axis=0)
            v_aug = jnp.concatenate([v_ref[j, ls, :], ones], axis=0)
            for h in (2 * pair, 2 * pair + 1):
                s = _dot(qa_ref[h, r0:, :], ka)
                if mask is not None:
                    s = jnp.where(mask, s, -jnp.inf)
                parts = [s[:, t * LANES:(t + 1) * LANES] for t in range(tk // LANES)]
                m_old = m_ref[h, r0:, :]
                m_new = jnp.maximum(m_old, jnp.max(functools.reduce(jnp.maximum, parts), axis=-1, keepdims=True))
                alpha = jnp.exp2(m_old - m_new)
                p = jnp.concatenate([jnp.exp2(t - m_new) for t in parts], axis=1).astype(BF16)
                m_ref[h, r0:, :] = m_new
                acc_ref[h, r0:, :] = (jnp.concatenate([alpha, alpha], axis=1) * acc_ref[h, r0:, :]
                                      + _dot_nt(p, v_aug))

    def body(j, carry):
        block(j, 0, None)
        return carry

    n_full = i * (tq // tk)
    lax.fori_loop(0, n_full, body, 0)
    for d in range(tq // tk):
        rows = tq - d * tk
        row = lax.broadcasted_iota(jnp.int32, (rows, tk), 0)
        col = lax.broadcasted_iota(jnp.int32, (rows, tk), 1)
        block(n_full + d, d * tk, col <= row)
    for pair in range(HEADS // 2):
        lo = acc_ref[2 * pair, :, 0:LANES] / acc_ref[2 * pair, :, LANES:2 * LANES]
        hi = acc_ref[2 * pair + 1, :, 0:LANES] / acc_ref[2 * pair + 1, :, LANES:2 * LANES]
        o_ref[0, :, pair * LANES:(pair + 1) * LANES] = jnp.where(
            lax.shift_right_logical(lane, 6) == 0, lo, hi).astype(BF16)


def _fox_prompt(q, cq, ktb, vtb, ckt):
    b, s, _ = q.shape
    tq = min(QUERY_BLOCK, s)
    assert s % tq == 0 and tq % KV_BLOCK == 0
    nkb = s // KV_BLOCK
    blk = lambda w: pl.BlockSpec((1, tq, w), lambda bi, i: (bi, i, 0))
    full = lambda r: pl.BlockSpec((nkb, r, KV_BLOCK), lambda bi, i: (bi, 0, 0))
    return pl.pallas_call(
        functools.partial(_fox_prompt_kernel, tq=tq),
        out_shape=jax.ShapeDtypeStruct((b, s, WIDTH), BF16),
        grid=(b, s // tq),
        in_specs=[blk(WIDTH), blk(LANES), full(WIDTH), full(WIDTH), full(LANES)],
        out_specs=blk(WIDTH),
        scratch_shapes=[pltpu.VMEM((HEADS, tq, 2 * LANES), BF16),
                        pltpu.VMEM((HEADS, tq, LANES), F32),
                        pltpu.VMEM((HEADS, tq, 2 * LANES), F32)],
        compiler_params=_params(("arbitrary", "arbitrary")),
        name="fox_prompt",
    )(q, cq, ktb, vtb, ckt)


def _fox_sample_kernel(pt_ref, q_ref, kn_ref, vn_ref, ckn_ref, *refs, t_new, pps, page):
    k_refs = refs[0:pps]
    v_refs = refs[pps:2 * pps]
    lf_refs = refs[2 * pps:3 * pps]
    o_ref = refs[3 * pps]
    qrow_ref, m_ref, l_ref, acc_ref, run_ref, pad_ref = refs[3 * pps + 1:]
    del pt_ref
    step = pl.program_id(1)
    n_rows = t_new * HEADS
    rowh = lax.broadcasted_iota(jnp.int32, (n_rows, WIDTH), 0) & (HEADS - 1)
    laneh = lax.shift_right_logical(lax.broadcasted_iota(jnp.int32, (n_rows, WIDTH), 1), 6)
    head_mask = rowh == laneh

    @pl.when(step == 0)
    def _():
        q = q_ref[0]
        rows = jnp.concatenate([jnp.broadcast_to(q[t:t + 1, :], (HEADS, WIDTH)) for t in range(t_new)],
                               axis=0)
        qrow_ref[...] = jnp.where(head_mask, rows, 0.0).astype(BF16)
        m_ref[...] = jnp.full(m_ref.shape, -jnp.inf, F32)
        l_ref[...] = jnp.zeros(l_ref.shape, F32)
        acc_ref[...] = jnp.zeros(acc_ref.shape, F32)
        run_ref[...] = jnp.zeros(run_ref.shape, F32)

    qrows = qrow_ref[...]
    lf = jnp.concatenate([r[0] for r in lf_refs], axis=0)
    newer = lax.broadcasted_iota(jnp.int32, (page, page), 0) > lax.broadcasted_iota(jnp.int32, (page, page), 1)
    suf = _dot3_rhs(lf, jnp.where(newer, 1.0, 0.0).astype(BF16))
    run = run_ref[...]
    biases = []
    for n in range(pps):
        blk = slice(n * HEADS, (n + 1) * HEADS)
        biases.append(jnp.concatenate([suf[blk] + run] * t_new, axis=0))
        run = run + (suf[blk][:, 0:1] + lf[blk][:, 0:1])
    run_ref[...] = run
    pair = lambda refs, n: jnp.concatenate([refs[n][0].astype(BF16), refs[n + 1][0].astype(BF16)], axis=1)
    scores = [_dot(qrows, pair(k_refs, n)) + jnp.concatenate(biases[n:n + 2], axis=1)
              for n in range(0, pps, 2)]
    m_old = m_ref[...]
    m_new = jnp.maximum(m_old, jnp.max(functools.reduce(jnp.maximum, scores), axis=-1, keepdims=True))
    alpha = jnp.exp(m_old - m_new)
    ps = [jnp.exp(s - m_new) for s in scores]
    pv = functools.reduce(lambda a, b: a + b,
                          [_dot_nt(p.astype(BF16), pair(v_refs, 2 * i)) for i, p in enumerate(ps)])
    m_ref[...] = m_new
    l_ref[...] = alpha * l_ref[...] + jnp.sum(functools.reduce(lambda a, b: a + b, ps), axis=-1, keepdims=True)
    acc_ref[...] = alpha * acc_ref[...] + pv

    @pl.when(step == pl.num_programs(1) - 1)
    def _():
        pad_ref[...] = jnp.zeros(pad_ref.shape, F32)
        pad_ref[0, 0:t_new, :] = kn_ref[0]
        pad_ref[1, 0:t_new, :] = vn_ref[0]
        pad_ref[2, 0:t_new, 0:LANES] = ckn_ref[0]
        kn = pad_ref[0].astype(BF16)
        vn = pad_ref[1].astype(BF16)
        ckn = pad_ref[2, :, 0:LANES].astype(BF16)
        r8 = lax.broadcasted_iota(jnp.int32, (n_rows, LANES), 0) & (HEADS - 1)
        l8 = lax.broadcasted_iota(jnp.int32, (n_rows, LANES), 1)
        sel = (lax.shift_right_logical(l8, 3) == r8) & ((l8 & (BIAS_STRIDE - 1)) >= BIAS_PARTS) \
            & ((l8 & (BIAS_STRIDE - 1)) < 2 * BIAS_PARTS)
        cqr = jnp.where(sel, 1.0, 0.0).astype(BF16)
        s = _dot_nt(jnp.concatenate([qrows, cqr], axis=1), jnp.concatenate([kn, ckn], axis=1))
        key = lax.broadcasted_iota(jnp.int32, (n_rows, page), 1)
        tok = lax.shift_right_logical(lax.broadcasted_iota(jnp.int32, (n_rows, page), 0), 3)
        s = jnp.where(key <= tok, s, -jnp.inf)
        _, l, acc = _softmax_step((m_ref[...], l_ref[...], acc_ref[...]), s, vn)
        o = jnp.where(head_mask, acc / l, 0.0)
        o_ref[0] = jnp.sum(o.reshape(t_new, HEADS, WIDTH), axis=1).astype(BF16)


def _fox_sample(q, kb, vb, ck, cache_k, cache_v, cache_lf_t, page_table):
    db, t_new, _ = q.shape
    n_pages = page_table.shape[1]
    n_pool, _, page = cache_k.shape
    pps = 32
    assert n_pages % pps == 0 and pps % 2 == 0 and page == LANES and t_new <= HEADS
    n_rows = t_new * HEADS
    pt = page_table.reshape(-1)

    def page_idx(n):
        return lambda b, s, pt_ref: (pt_ref[b * n_pages + n_pages - 1 - (s * pps + n)], 0, 0)

    new = lambda w: pl.BlockSpec((1, t_new, w), lambda b, s, pt_ref: (b, 0, 0))
    in_specs = ([new(WIDTH), new(WIDTH), new(WIDTH), new(LANES)]
                + [pl.BlockSpec((1, WIDTH, page), page_idx(n)) for n in range(pps)]
                + [pl.BlockSpec((1, WIDTH, page), page_idx(n)) for n in range(pps)]
                + [pl.BlockSpec((1, HEADS, page), page_idx(n)) for n in range(pps)])
    grid_spec = pltpu.PrefetchScalarGridSpec(
        num_scalar_prefetch=1,
        grid=(db, n_pages // pps),
        in_specs=in_specs,
        out_specs=new(WIDTH),
        scratch_shapes=[pltpu.VMEM((n_rows, WIDTH), BF16),
                        pltpu.VMEM((n_rows, 1), F32), pltpu.VMEM((n_rows, 1), F32),
                        pltpu.VMEM((n_rows, WIDTH), F32), pltpu.VMEM((HEADS, 1), F32),
                        pltpu.VMEM((3, page, WIDTH), F32)],
    )
    return pl.pallas_call(
        functools.partial(_fox_sample_kernel, t_new=t_new, pps=pps, page=page),
        out_shape=jax.ShapeDtypeStruct((db, t_new, WIDTH), BF16),
        grid_spec=grid_spec,
        compiler_params=_params(("arbitrary", "arbitrary")),
        name="fox_sample",
    )(pt, q, kb, vb, ck, *([cache_k] * pps), *([cache_v] * pps), *([cache_lf_t] * pps))


def _mlp_kernel(x_ref, or_ref, of_ref, gt_ref, lg_ref, lb_ref, wb_ref, wo_ref, g1_ref, b1_ref,
                wu_ref, wd_ref, g2_ref, b2_ref, y_ref, *, alpha, ff_chunk):
    tm, d = x_ref.shape
    n_sub = 2 if tm >= 256 else 1
    rows = tm // n_sub

    def half(si):
        rs = slice(si * rows, (si + 1) * rows)
        hp = _layer_norm(x_ref[rs, :], lg_ref[...], lb_ref[...])
        gt = gt_ref[rs, :]
        yield
        merged = (gt[:, :d].astype(F32) * _dot(or_ref[rs, :], wb_ref[0:WIDTH, :])
                  + gt[:, d:].astype(F32) * _dot(of_ref[rs, :], wb_ref[WIDTH:2 * WIDTH, :]))
        yield
        pre = alpha * hp + _dot(merged.astype(BF16), wo_ref[...])
        yield
        x1 = _layer_norm(pre, g1_ref[...], b1_ref[...])
        x1b = x1.astype(BF16)
        yield
        h = jnp.zeros_like(x1)
        for c in range(wu_ref.shape[1] // ff_chunk):
            cs = slice(c * ff_chunk, (c + 1) * ff_chunk)
            up = jnp.maximum(_dot(x1b, wu_ref[:, cs]), 0.0)
            yield
            h = h + _dot((up * up).astype(BF16), wd_ref[cs, :])
            yield
        y_ref[rs, :] = _layer_norm(alpha * x1 + h, g2_ref[...], b2_ref[...])

    _round_robin([half(si) for si in range(n_sub)], stagger=4)


def _mlp(x2d, o_r, o_f, gates, ln_g, ln_b, wb, wo, g1, b1, wu, wd, g2, b2, alpha):
    m, d = x2d.shape
    tm = min(512, m)
    assert m % tm == 0
    row = lambda w: pl.BlockSpec((tm, w), lambda i: (i, 0))
    consts = (ln_g, ln_b, wb, wo, g1, b1, wu, wd, g2, b2)
    return pl.pallas_call(
        functools.partial(_mlp_kernel, alpha=alpha, ff_chunk=1024),
        out_shape=jax.ShapeDtypeStruct((m, d), F32),
        grid=(m // tm,),
        in_specs=[row(d), row(WIDTH), row(WIDTH), row(2 * d)] + [_const_spec(c.shape) for c in consts],
        out_specs=row(d),
        compiler_params=_params(("arbitrary",)),
        name="mlp",
    )(x2d, o_r, o_f, gates, *consts)


def kernel(x_prompt, x_sample, state_wkv, state_shift, cache_k, cache_v, cache_logf, page_table, ln_in_g, ln_in_b, w_in, rwkv_mu, rwkv_w0, rwkv_w2, rwkv_a0, rwkv_a2, rwkv_g2, rwkv_k_k, rwkv_k_a, rwkv_r_k, rwkv_gn_g, rwkv_gn_b, fox_b_f, w_branch, w_out, ln1_g, ln1_b, w_up, w_down, ln2_g, ln2_b):
    depth = w_in.shape[0]
    assert depth == 1, "the entry LayerNorm is fused into the first layer's kernels"
    bp, seq, d = x_prompt.shape
    db, dseq, _ = x_sample.shape
    n_pool, page = cache_k.shape[1], cache_k.shape[2]
    alpha = (2.0 * depth) ** 0.25
    row2 = lambda t: t.reshape(1, -1).astype(F32)
    ln_g, ln_b = row2(ln_in_g), row2(ln_in_b)

    l = 0
    w = w_in[l]
    fox_off = RWKV_PROJ
    wr = w[:, :RWKV_PROJ].astype(BF16)
    wq = w[:, fox_off:fox_off + WIDTH].astype(BF16)
    wt = w.T
    wkt = wt[fox_off + WIDTH:fox_off + 2 * WIDTH].astype(BF16)
    wvt = wt[fox_off + 2 * WIDTH:fox_off + 3 * WIDTH].astype(BF16)
    wft = jnp.pad(wt[fox_off + 3 * WIDTH:fox_off + 3 * WIDTH + HEADS], ((0, LANES - HEADS), (0, 0))).astype(BF16)
    wg = w[:, fox_off + 3 * WIDTH + HEADS:].astype(BF16)
    bf_col = jnp.pad(fox_b_f[l].reshape(HEADS, 1), ((0, LANES - HEADS), (0, 0))).astype(F32)
    proj_consts = (ln_g, ln_b, wq, wkt, wvt, wft, bf_col, wg)
    zl = jnp.zeros((DECAY_LORA, WIDTH), BF16)
    w2p = jnp.concatenate([rwkv_w2[l].astype(BF16), zl], axis=0)
    a2p = jnp.concatenate([zl, rwkv_a2[l].astype(BF16)], axis=0)
    rwkv_consts = (ln_g, ln_b, wr, row2(rwkv_mu[l]), row2(rwkv_w0[l]), w2p, row2(rwkv_a0[l]), a2p,
                   rwkv_g2[l].astype(BF16), row2(rwkv_k_k[l]), row2(rwkv_k_a[l]), row2(rwkv_r_k[l]),
                   row2(rwkv_gn_g[l]), row2(rwkv_gn_b[l]))
    mlp_consts = (ln_g, ln_b, w_branch[l].astype(BF16), w_out[l].astype(BF16), row2(ln1_g[l]),
                  row2(ln1_b[l]), w_up[l].astype(BF16), w_down[l].astype(BF16), row2(ln2_g[l]),
                  row2(ln2_b[l]))

    xp2 = x_prompt.reshape(bp * seq, d)
    q, kt32, vt32, ktb, vtb, lft, cq, ckt, gates = _proj(xp2, *proj_consts, seq, LOG2_E)
    to_seq = lambda t: t.reshape(bp, seq, t.shape[-1])
    o_f = _fox_prompt(to_seq(q), to_seq(cq), ktb, vtb, ckt)
    o_r, p_shift, p_wkv = _rwkv(x_prompt, seq, *rwkv_consts,
                                jnp.zeros((bp, RWKV_PROJ), F32),
                                jnp.zeros((bp, HEADS, HEAD_DIM, HEAD_DIM), F32))
    y_prompt = _mlp(xp2, o_r.reshape(bp * seq, WIDTH), o_f.reshape(bp * seq, WIDTH), gates,
                    *mlp_consts, alpha).reshape(bp, seq, d)
    heads_last = lambda t: jnp.transpose(t.reshape(bp, HEADS, HEAD_DIM, seq), (0, 3, 1, 2))[None]
    p_k = heads_last(kt32)
    p_v = heads_last(vt32)
    p_lf = jnp.swapaxes(lft, 1, 2)[None]

    xs2 = x_sample.reshape(db * dseq, d)
    q, kt32, vt32, _, _, lft, cq, ckt, gates = _proj(xs2, *proj_consts, dseq, 1.0)
    rows = lambda t: jnp.swapaxes(t, 1, 2).reshape(db * dseq, t.shape[1])
    k32, v32, lf, ck = rows(kt32), rows(vt32), rows(lft), rows(ckt)
    to_seq = lambda t: t.reshape(db, dseq, t.shape[-1])
    o_f = _fox_sample(to_seq(q.astype(F32)), to_seq(k32), to_seq(v32), to_seq(ck.astype(F32)),
                      jnp.transpose(cache_k[l], (0, 2, 3, 1)).reshape(n_pool, WIDTH, page),
                      jnp.transpose(cache_v[l], (0, 2, 3, 1)).reshape(n_pool, WIDTH, page),
                      jnp.swapaxes(cache_logf[l], 1, 2), page_table)
    xs_pad = jnp.pad(x_sample, ((0, 0), (0, CHUNK - dseq), (0, 0)))
    o_r, s_shift, s_wkv = _rwkv(xs_pad, dseq, *rwkv_consts, state_shift[l], state_wkv[l])
    y_sample = _mlp(xs2, o_r[:, :dseq].reshape(db * dseq, WIDTH), o_f.reshape(db * dseq, WIDTH), gates,
                    *mlp_consts, alpha).reshape(db, dseq, d)
    s_k = k32.reshape(1, db, dseq, HEADS, HEAD_DIM)
    s_v = v32.reshape(1, db, dseq, HEADS, HEAD_DIM)
    s_lf = lf.reshape(1, db, dseq, HEADS)

    return (y_prompt, y_sample, p_wkv[None], p_shift[None], p_k, p_v, p_lf,
            s_wkv[None], s_shift[None], s_k, s_v, s_lf)
```
